```python
import jax
import jax.numpy as jnp
from jax import lax
import numpy as np

D_MODEL = 2048
BATCH = 8
SEQ = 2048
DEPTH = 2
DEC_BATCH = 128
DEC_SEQ = 1
PAST_LEN = 8192
PAGE_SIZE = 128

PLE_DIM = 256
D_MIX = D_MODEL
A_DK = 128
A_DV = 128
A_HEADS = (D_MIX // 2) // A_DV
A_WIDTH = A_HEADS * A_DV
HGRN_CHUNK = 64
B_DH = 64
B_HEADS = (D_MIX // 4) // B_DH
B_KV_HEADS = max(1, B_HEADS // 4)
B_GROUP = B_HEADS // B_KV_HEADS
B_WIDTH = B_HEADS * B_DH
WINDOW = 128
C_DG = 128
C_GROUPS = (D_MIX - A_WIDTH - B_WIDTH) // C_DG
C_WIDTH = C_GROUPS * C_DG
C_CHUNK = 128
D_FF = 256 * ((8 * D_MODEL // 3 + 255) // 256)
CONV_W = 3
EPS = 1e-6
D_IN = 4 * A_WIDTH + B_WIDTH + 2 * B_KV_HEADS * B_DH + 2 * C_WIDTH
SPLITS = (4 * A_WIDTH, 4 * A_WIDTH + B_WIDTH, 4 * A_WIDTH + B_WIDTH + B_KV_HEADS * B_DH, 4 * A_WIDTH + B_WIDTH + 2 * B_KV_HEADS * B_DH)

kernel_name = 'hymba_hgrn2_swa_gmlp_convffn_step'


def rms_norm(x, g):
    xf = x.astype(jnp.float32)
    y = xf * lax.rsqrt(jnp.mean(xf * xf, -1, keepdims=True) + EPS)
    return (y * g.astype(jnp.float32)).astype(x.dtype)


def layer_norm(x, g, b):
    xf = x.astype(jnp.float32)
    xc = xf - jnp.mean(xf, -1, keepdims=True)
    y = xc * lax.rsqrt(jnp.mean(xc * xc, -1, keepdims=True) + EPS)
    return (y * g.astype(jnp.float32) + b.astype(jnp.float32)).astype(x.dtype)


def hgrn2_lower_bounds(lb_logits):
    sm = jax.nn.softmax(lb_logits.astype(jnp.float32), axis=0)
    cs = jnp.cumsum(sm, axis=0)
    return cs - cs[0:1]


def hgrn2_chunked(q, log_f, k, v, s0):
    bsz, t, h, dk = q.shape
    dv = v.shape[-1]
    c = HGRN_CHUNK if t >= HGRN_CHUNK else t
    n = -(-t // c)
    pad = n * c - t

    def prep(a):
        a = jnp.pad(a, ((0, 0), (0, pad), (0, 0), (0, 0)))
        return a.reshape(bsz, n, c, h, a.shape[-1]).transpose(1, 0, 3, 2, 4)

    causal = jnp.tril(jnp.ones((c, c), bool))[:, :, None]

    def step(s, blk):
        qc, lfc, kc, vc = blk
        b = jnp.cumsum(lfc, axis=2)
        o_inter = jnp.einsum('bhtk,bhkv->bhtv', qc * jnp.exp(b), s)
        diff = b[:, :, :, None, :] - b[:, :, None, :, :]
        decay = jnp.exp(jnp.where(causal, diff, -jnp.inf))
        scores = jnp.einsum('bhtk,bhsk,bhtsk->bhts', qc, kc, decay)
        o_intra = jnp.einsum('bhts,bhsv->bhtv', scores, vc)
        b_last = b[:, :, -1:, :]
        s_new = jnp.exp(b_last[:, :, 0, :])[..., None] * s + jnp.einsum('bhsk,bhsv->bhkv', kc * jnp.exp(b_last - b), vc)
        return s_new, o_inter + o_intra

    s_fin, o = lax.scan(step, s0, (prep(q), prep(log_f), prep(k), prep(v)))
    o = o.transpose(1, 0, 3, 2, 4).reshape(bsz, n * c, h, dv)[:, :t]
    return o, s_fin


def mixer_hgrn2(za, lb, norm_g, s0):
    bsz, t, _ = za.shape
    zq, zf, zi, zg = jnp.split(za, 4, axis=-1)

    def heads(a):
        return a.reshape(bsz, t, A_HEADS, -1).astype(jnp.float32)

    q = jax.nn.silu(heads(zq))
    lbh = lb.reshape(A_HEADS, A_DK)
    log_f = jnp.logaddexp(jnp.log(lbh), jnp.log1p(-lbh) + jax.nn.log_sigmoid(heads(zf)))
    k = -jnp.expm1(log_f)
    v = heads(zi)
    o, s_fin = hgrn2_chunked(q, log_f, k, v, s0.astype(jnp.float32))
    o = rms_norm(o, norm_g) * jax.nn.silu(heads(zg))
    return o.reshape(bsz, t, A_WIDTH).astype(za.dtype), s_fin.astype(za.dtype)


def sink_attend(q, k, v, mask, sink):
    s = jnp.einsum('bnhgqd,bnhkd->bnhgqk', q.astype(jnp.float32), k.astype(jnp.float32)) * (B_DH ** -0.5)
    s = jnp.where(mask, s, -jnp.inf)
    sink_b = sink.astype(jnp.float32).reshape(B_KV_HEADS, B_GROUP)[:, :, None, None]
    m = jnp.maximum(jnp.max(s, -1, keepdims=True), sink_b)
    p = jnp.exp(s - m)
    denom = jnp.sum(p, -1, keepdims=True) + jnp.exp(sink_b - m)
    return jnp.einsum('bnhgqk,bnhkd->bnhgqd', p, v.astype(jnp.float32)) / denom


def swa_prompt(q, k, v, sink):
    bsz, t = q.shape[:2]
    nb = t // WINDOW
    qb = q.reshape(bsz, nb, WINDOW, B_KV_HEADS, B_GROUP, B_DH).transpose(0, 1, 3, 4, 2, 5)

    def blocks(a):
        a = a.reshape(bsz, nb, WINDOW, B_KV_HEADS, B_DH).transpose(0, 1, 3, 2, 4)
        prev = jnp.pad(a, ((0, 0), (1, 0), (0, 0), (0, 0), (0, 0)))[:, :-1]
        return jnp.concatenate([prev, a], axis=3)

    qi = jnp.arange(WINDOW)[:, None]
    kj = jnp.arange(2 * WINDOW)[None, :]
    band = (kj > qi) & (kj <= qi + WINDOW)
    not_first = jnp.arange(nb)[:, None, None] > 0
    mask = (band[None] & (not_first | (kj >= WINDOW)[None]))[:, None, None]
    o = sink_attend(qb, blocks(k), blocks(v), mask, sink)
    return o.transpose(0, 1, 4, 2, 3, 5).reshape(bsz, t, B_WIDTH)


def swa_decode(q, k, v, ck, cv, sink):
    bsz, t = q.shape[:2]
    kk = jnp.concatenate([ck.astype(k.dtype), k], axis=1)
    vv = jnp.concatenate([cv.astype(v.dtype), v], axis=1)
    qb = q.reshape(bsz, t, B_KV_HEADS, B_GROUP, B_DH).transpose(0, 2, 3, 1, 4)[:, None]
    kb = kk.transpose(0, 2, 1, 3)[:, None]
    vb = vv.transpose(0, 2, 1, 3)[:, None]
    qi = jnp.arange(t)[:, None]
    kj = jnp.arange(WINDOW + t)[None, :]
    mask = ((kj > qi) & (kj <= qi + WINDOW))[None, None, None]
    o = sink_attend(qb, kb, vb, mask, sink)
    o = o[:, 0].transpose(0, 3, 1, 2, 4).reshape(bsz, t, B_WIDTH)
    return o, kk[:, -WINDOW:], vv[:, -WINDOW:]


def chunk_mix(v, ws, bs):
    bsz, t = v.shape[:2]
    n = -(-t // C_CHUNK)
    pad = n * C_CHUNK - t
    vp = jnp.pad(v, ((0, 0), (0, pad), (0, 0), (0, 0))).reshape(bsz, n, C_CHUNK, C_GROUPS, C_DG)
    w = jnp.where(jnp.tril(jnp.ones((C_CHUNK, C_CHUNK), bool)), ws, 0.0).astype(v.dtype)
    mixed = jnp.einsum('gpq,bnqgc->bnpgc', w, vp) + bs.T[:, :, None].astype(v.dtype)
    return mixed.reshape(bsz, n * C_CHUNK, C_GROUPS, C_DG)[:, :t]


def mixer_gmlp(zc, ln_g, ln_b, ws, bs):
    bsz, t, _ = zc.shape
    u, vr = jnp.split(jax.nn.gelu(zc), 2, axis=-1)
    vn = layer_norm(vr, ln_g, ln_b).reshape(bsz, t, C_GROUPS, C_DG)
    out = u * chunk_mix(vn, ws, bs).reshape(bsz, t, C_WIDTH)
    return out, vn


def conv_ffn(h, prev, w_up, conv_w, conv_b, w_down):
    up = h @ w_up
    t = up.shape[1]
    xp = jnp.concatenate([prev.astype(up.dtype), up], axis=1)
    conv = conv_b + conv_w[0] * xp[:, 0:t]
    for j in range(1, CONV_W):
        conv = conv + conv_w[j] * xp[:, j:j + t]
    gate, val = jnp.split(conv, 2, axis=-1)
    return (jax.nn.silu(gate) * val) @ w_down, xp[:, -(CONV_W - 1):]


def trunk_layer(x, p, lb, st_hgrn, st_k, st_v, st_conv, w, decode):
    bsz, t, _ = x.shape
    h = rms_norm(x, w['norm1_g'])
    z = h @ w['w_in']
    za, zq, zk, zv, zc = jnp.split(z, SPLITS, axis=-1)
    o_a, hgrn_new = mixer_hgrn2(za, lb, w['hgrn_norm_g'], st_hgrn)
    q = rms_norm(zq.reshape(bsz, t, B_HEADS, B_DH), w['q_norm_g'])
    k = rms_norm(zk.reshape(bsz, t, B_KV_HEADS, B_DH), w['k_norm_g'])
    v = zv.reshape(bsz, t, B_KV_HEADS, B_DH)
    if decode:
        o_b, k_new, v_new = swa_decode(q, k, v, st_k, st_v, w['swa_sinks'])
    else:
        o_b = swa_prompt(q, k, v, w['swa_sinks'])
        k_new, v_new = k[:, -WINDOW:], v[:, -WINDOW:]
    o_c, c_rows = mixer_gmlp(zc, w['gmlp_ln_g'], w['gmlp_ln_b'], w['gmlp_ws'], w['gmlp_bs'])
    mix = jnp.concatenate([o_a, o_b.astype(x.dtype), o_c.astype(x.dtype)], axis=-1)
    x = x + mix @ w['w_out']
    f, conv_new = conv_ffn(rms_norm(x, w['norm2_g']), st_conv, w['w_up'], w['conv_w'], w['conv_b'], w['w_down'])
    x = x + f
    x = x + jax.nn.sigmoid(rms_norm(x, w['ple_norm_g']) @ w['w_ple_gate']) * (p @ w['w_ple_proj'])
    return x, hgrn_new, k_new, v_new, c_rows, conv_new


def setup_inputs(seed: int = 0) -> dict:
    key = jax.random.key(seed)
    ks = jax.random.split(key, 28)

    def nrm(k, shape, scale=1.0):
        return scale * jax.random.normal(k, shape, jnp.float32)

    return {
        'x_prompt': nrm(ks[0], (BATCH, SEQ, D_MODEL)),
        'x_sample': nrm(ks[1], (DEC_BATCH, DEC_SEQ, D_MODEL)),
        'state_hgrn': nrm(ks[2], (DEPTH, DEC_BATCH, A_HEADS, A_DK, A_DV), 0.5),
        'cache_swa_k': nrm(ks[3], (DEPTH, DEC_BATCH, WINDOW, B_KV_HEADS, B_DH)),
        'cache_swa_v': nrm(ks[4], (DEPTH, DEC_BATCH, WINDOW, B_KV_HEADS, B_DH)),
        'state_ffn_conv': nrm(ks[5], (DEPTH, DEC_BATCH, CONV_W - 1, 2 * D_FF)),
        'p_prompt': nrm(ks[6], (DEPTH, BATCH, SEQ, PLE_DIM)),
        'p_sample': nrm(ks[7], (DEPTH, DEC_BATCH, DEC_SEQ, PLE_DIM)),
        'norm1_g': 1.0 + nrm(ks[8], (DEPTH, D_MODEL), 0.1),
        'w_in': nrm(ks[9], (DEPTH, D_MODEL, D_IN), D_MODEL ** -0.5),
        'hgrn_lb_logits': nrm(ks[10], (DEPTH, A_HEADS * A_DK)),
        'hgrn_norm_g': 1.0 + nrm(ks[11], (DEPTH, A_DV), 0.1),
        'q_norm_g': 1.0 + nrm(ks[12], (DEPTH, B_DH), 0.1),
        'k_norm_g': 1.0 + nrm(ks[13], (DEPTH, B_DH), 0.1),
        'swa_sinks': nrm(ks[14], (DEPTH, B_HEADS), 0.5),
        'gmlp_ln_g': 1.0 + nrm(ks[15], (DEPTH, C_WIDTH), 0.1),
        'gmlp_ln_b': nrm(ks[16], (DEPTH, C_WIDTH), 0.02),
        'gmlp_ws': nrm(ks[17], (DEPTH, C_GROUPS, C_CHUNK, C_CHUNK), C_CHUNK ** -0.5),
        'gmlp_bs': 1.0 + nrm(ks[18], (DEPTH, C_GROUPS, C_CHUNK), 0.1),
        'w_out': nrm(ks[19], (DEPTH, D_MIX, D_MODEL), D_MIX ** -0.5),
        'norm2_g': 1.0 + nrm(ks[20], (DEPTH, D_MODEL), 0.1),
        'w_up': nrm(ks[21], (DEPTH, D_MODEL, 2 * D_FF), D_MODEL ** -0.5),
        'conv_w': nrm(ks[22], (DEPTH, CONV_W, 2 * D_FF), CONV_W ** -0.5),
        'conv_b': nrm(ks[23], (DEPTH, 2 * D_FF), 0.02),
        'w_down': nrm(ks[24], (DEPTH, D_FF, D_MODEL), D_FF ** -0.5),
        'ple_norm_g': 1.0 + nrm(ks[25], (DEPTH, D_MODEL), 0.1),
        'w_ple_gate': nrm(ks[26], (DEPTH, D_MODEL, D_MODEL), D_MODEL ** -0.5),
        'w_ple_proj': nrm(ks[27], (DEPTH, PLE_DIM, D_MODEL), PLE_DIM ** -0.5),
    }


def reference(x_prompt, x_sample, state_hgrn, cache_swa_k, cache_swa_v, state_ffn_conv, p_prompt, p_sample,
              norm1_g, w_in, hgrn_lb_logits, hgrn_norm_g, q_norm_g, k_norm_g, swa_sinks, gmlp_ln_g, gmlp_ln_b,
              gmlp_ws, gmlp_bs, w_out, norm2_g, w_up, conv_w, conv_b, w_down, ple_norm_g, w_ple_gate, w_ple_proj):
    lbs = hgrn2_lower_bounds(hgrn_lb_logits)
    xp, xs = x_prompt, x_sample
    hp, hs, kp, vp, ksm, vsm, gs, cp, cs = [], [], [], [], [], [], [], [], []
    for l in range(DEPTH):
        w = {
            'norm1_g': norm1_g[l], 'w_in': w_in[l], 'hgrn_norm_g': hgrn_norm_g[l],
            'q_norm_g': q_norm_g[l], 'k_norm_g': k_norm_g[l], 'swa_sinks': swa_sinks[l],
            'gmlp_ln_g': gmlp_ln_g[l], 'gmlp_ln_b': gmlp_ln_b[l], 'gmlp_ws': gmlp_ws[l], 'gmlp_bs': gmlp_bs[l],
            'w_out': w_out[l], 'norm2_g': norm2_g[l], 'w_up': w_up[l], 'conv_w': conv_w[l],
            'conv_b': conv_b[l], 'w_down': w_down[l], 'ple_norm_g': ple_norm_g[l],
            'w_ple_gate': w_ple_gate[l], 'w_ple_proj': w_ple_proj[l],
        }
        zero_h = jnp.zeros((xp.shape[0], A_HEADS, A_DK, A_DV), jnp.float32)
        zero_c = jnp.zeros((xp.shape[0], CONV_W - 1, 2 * D_FF), xp.dtype)
        xp, h_new, k_new, v_new, _, c_new = trunk_layer(xp, p_prompt[l], lbs[l], zero_h, None, None, zero_c, w, False)
        hp.append(h_new); kp.append(k_new); vp.append(v_new); cp.append(c_new)
        xs, h_new, k_new, v_new, g_new, c_new = trunk_layer(xs, p_sample[l], lbs[l], state_hgrn[l], cache_swa_k[l],
                                                            cache_swa_v[l], state_ffn_conv[l], w, True)
        hs.append(h_new); ksm.append(k_new); vsm.append(v_new); gs.append(g_new); cs.append(c_new)
    return (xp, xs, jnp.stack(hp), jnp.stack(hs), jnp.stack(kp), jnp.stack(vp), jnp.stack(ksm), jnp.stack(vsm),
            jnp.stack(gs), jnp.stack(cp), jnp.stack(cs))
```

```python
import functools

import numpy as np
import jax
import jax.numpy as jnp
from jax import lax
from jax.experimental import pallas as pl
from jax.experimental.pallas import tpu as pltpu

D_MODEL = 2048
PLE_DIM = 256
A_DK = 128
A_DV = 128
A_HEADS = 8
A_WIDTH = A_HEADS * A_DV
B_DH = 64
B_HEADS = 8
B_KV_HEADS = 2
B_GROUP = B_HEADS // B_KV_HEADS
B_WIDTH = B_HEADS * B_DH
WINDOW = 128
C_DG = 128
C_GROUPS = 4
C_WIDTH = C_GROUPS * C_DG
C_CHUNK = 128
D_FF = 5632
CONV_W = 3
EPS = 1e-6
D_IN = 4 * A_WIDTH + B_WIDTH + 2 * B_KV_HEADS * B_DH + 2 * C_WIDTH
KV_WIDTH = B_KV_HEADS * B_DH
OFF_Q, OFF_F, OFF_I, OFF_G = 0, A_WIDTH, 2 * A_WIDTH, 3 * A_WIDTH
OFF_SQ = 4 * A_WIDTH
OFF_SK = OFF_SQ + B_WIDTH
OFF_SV = OFF_SK + KV_WIDTH
OFF_C = OFF_SV + KV_WIDTH

BLK = 128
N_LEVELS = 7
VMEM_LIMIT = 56 * 1024 * 1024

F32 = jnp.float32
BF16 = jnp.bfloat16

_NT = (((1,), (1,)), ((), ()))
_TN = (((0,), (0,)), ((), ()))


def _dot(a, b):
    return jnp.dot(a, b, preferred_element_type=F32)


def _dot_nt(a, b):
    return lax.dot_general(a, b, _NT, preferred_element_type=F32)


def _decay_matrices():
    t = np.arange(BLK)[:, None]
    u = np.arange(BLK)[None, :]
    mats = [(u <= t), (u > t)]
    for k in range(N_LEVELS):
        n = 1 << k
        r = (t // (2 * n)) * (2 * n) + n - 1
        mats.append((u > np.minimum(t, r)) & (u <= np.maximum(t, r)))
    return np.concatenate(mats, axis=0).astype(np.float32)


_DECAY_A = _decay_matrices()
N_DECAY_ROWS = _DECAY_A.shape[0]


def _sigmoid(x):
    return jax.nn.sigmoid(x)


def _silu(x):
    return x * jax.nn.sigmoid(x)


def _log_sigmoid(x):
    return -(jnp.maximum(-x, 0.0) + jnp.log1p(jnp.exp(-jnp.abs(x))))


def _logaddexp(a, b):
    return jnp.maximum(a, b) + jnp.log1p(jnp.exp(-jnp.abs(a - b)))


def _gelu_tanh(x):
    c = np.float32(np.sqrt(2.0 / np.pi))
    return 0.5 * x * (1.0 + jnp.tanh(c * (x + np.float32(0.044715) * (x * x * x))))


def _rms(x, g):
    return x * lax.rsqrt(jnp.mean(x * x, axis=-1, keepdims=True) + EPS) * g


def _forget_gate(lbl_ref, layer, zf):
    lg = lbl_ref[...]
    e = jnp.exp(lg - jnp.max(lg, axis=0, keepdims=True))
    tot = jnp.sum(e, axis=0, keepdims=True)
    lb = jnp.zeros_like(tot)
    for j in range(1, layer + 1):
        lb = lb + e[j:j + 1, :] / tot
    logf = _logaddexp(jnp.log(lb), jnp.log1p(-lb) + _log_sigmoid(zf))
    return logf, (1.0 - lb) * _sigmoid(-zf)


def _segment_rms(x, g2, lane_lo):
    x2 = x * x
    s_lo = jnp.sum(jnp.where(lane_lo, x2, 0.0), axis=-1, keepdims=True)
    s_hi = jnp.sum(jnp.where(lane_lo, 0.0, x2), axis=-1, keepdims=True)
    inv = jnp.where(lane_lo, lax.rsqrt(s_lo / B_DH + EPS), lax.rsqrt(s_hi / B_DH + EPS))
    return x * inv * g2


def _layer_norm(x, g, b):
    xc = x - jnp.mean(x, axis=-1, keepdims=True)
    return xc * lax.rsqrt(jnp.mean(xc * xc, axis=-1, keepdims=True) + EPS) * g + b


def _in_proj_kernel(x_ref, g_ref, w_ref, o_ref):
    h = _rms(x_ref[...], g_ref[...]).astype(BF16)
    n = o_ref.shape[1]
    step = 512
    for c0 in range(0, n, step):
        c1 = min(c0 + step, n)
        o_ref[:, c0:c1] = _dot(h, w_ref[:, c0:c1])


def _in_proj(x, g, w, tm):
    m = x.shape[0]
    n = w.shape[1]
    return pl.pallas_call(
        _in_proj_kernel,
        grid=(m // tm,),
        in_specs=[
            pl.BlockSpec((tm, D_MODEL), lambda i: (i, 0)),
            pl.BlockSpec((1, D_MODEL), lambda i: (0, 0)),
            pl.BlockSpec((D_MODEL, n), lambda i: (0, 0), pipeline_mode=pl.Buffered(1)),
        ],
        out_specs=pl.BlockSpec((tm, n), lambda i: (i, 0)),
        out_shape=jax.ShapeDtypeStruct((m, n), F32),
        compiler_params=pltpu.CompilerParams(dimension_semantics=("parallel",), vmem_limit_bytes=VMEM_LIMIT),
        name="in_proj",
    )(x, g, w)


def _out_proj_kernel(mix_ref, x_ref, w_ref, o_ref):
    o_ref[...] = x_ref[...] + _dot(mix_ref[...].astype(BF16), w_ref[...])


def _out_proj(mix, x, w, tm):
    m = x.shape[0]
    return pl.pallas_call(
        _out_proj_kernel,
        grid=(m // tm,),
        in_specs=[
            pl.BlockSpec((tm, D_MODEL), lambda i: (i, 0)),
            pl.BlockSpec((tm, D_MODEL), lambda i: (i, 0)),
            pl.BlockSpec((D_MODEL, D_MODEL), lambda i: (0, 0), pipeline_mode=pl.Buffered(1)),
        ],
        out_specs=pl.BlockSpec((tm, D_MODEL), lambda i: (i, 0)),
        out_shape=jax.ShapeDtypeStruct((m, D_MODEL), F32),
        compiler_params=pltpu.CompilerParams(dimension_semantics=("parallel",), vmem_limit_bytes=VMEM_LIMIT),
        name="out_proj",
    )(mix, x, w)


def _ple_kernel(x_ref, p_ref, g_ref, wg_ref, wp_ref, o_ref):
    x = x_ref[...]
    h = _rms(x, g_ref[...]).astype(BF16)
    gate = _sigmoid(_dot(h, wg_ref[...]))
    o_ref[...] = x + gate * _dot(p_ref[...].astype(BF16), wp_ref[...])


def _ple(x, p, g, wg, wp, tm):
    m = x.shape[0]
    return pl.pallas_call(
        _ple_kernel,
        grid=(m // tm,),
        in_specs=[
            pl.BlockSpec((tm, D_MODEL), lambda i: (i, 0)),
            pl.BlockSpec((tm, PLE_DIM), lambda i: (i, 0)),
            pl.BlockSpec((1, D_MODEL), lambda i: (0, 0)),
            pl.BlockSpec((D_MODEL, D_MODEL), lambda i: (0, 0), pipeline_mode=pl.Buffered(1)),
            pl.BlockSpec((PLE_DIM, D_MODEL), lambda i: (0, 0), pipeline_mode=pl.Buffered(1)),
        ],
        out_specs=pl.BlockSpec((tm, D_MODEL), lambda i: (i, 0)),
        out_shape=jax.ShapeDtypeStruct((m, D_MODEL), F32),
        compiler_params=pltpu.CompilerParams(dimension_semantics=("parallel",), vmem_limit_bytes=VMEM_LIMIT),
        name="ple",
    )(x, p, g, wg, wp)


FFN_TN = 512
FFN_NJ = D_FF // FFN_TN
CARRY_ROWS = 8


def _ffn_prompt_kernel(x_ref, g_ref, wug_ref, wuv_ref, cwg_ref, cwv_ref, cbg_ref, cbv_ref, wd_ref,
                       o_ref, cg_ref, cv_ref, h_scr, acc_scr, carry_scr, bufg, bufv):
    i = pl.program_id(1)
    j = pl.program_id(2)
    tm = x_ref.shape[0]

    @pl.when(j == 0)
    def _():
        h_scr[...] = _rms(x_ref[...], g_ref[...]).astype(BF16)
        acc_scr[...] = jnp.zeros_like(acc_scr)

    @pl.when(i == 0)
    def _():
        carry_scr[0, j] = jnp.zeros((CARRY_ROWS, FFN_TN), F32)
        carry_scr[1, j] = jnp.zeros((CARRY_ROWS, FFN_TN), F32)

    h = h_scr[...]

    def conv(part, w_ref, cw_ref, cb_ref, buf, last_ref):
        up = _dot(h, w_ref[...])
        buf[0:CARRY_ROWS, :] = carry_scr[part, j]
        buf[CARRY_ROWS:, :] = up
        cw = cw_ref[...]
        out = cb_ref[...] + cw[0:1, :] * buf[CARRY_ROWS - 2:CARRY_ROWS - 2 + tm, :]
        out = out + cw[1:2, :] * buf[CARRY_ROWS - 1:CARRY_ROWS - 1 + tm, :]
        out = out + cw[2:3, :] * up
        carry_scr[part, j] = buf[tm:tm + CARRY_ROWS, :]
        last_ref[0] = up[tm - (CONV_W - 1):, :]
        return out

    gate = conv(0, wug_ref, cwg_ref, cbg_ref, bufg, cg_ref)
    val = conv(1, wuv_ref, cwv_ref, cbv_ref, bufv, cv_ref)
    act = (_silu(gate) * val).astype(BF16)
    acc_scr[...] += _dot(act, wd_ref[...])

    @pl.when(j == FFN_NJ - 1)
    def _():
        o_ref[...] = x_ref[...] + acc_scr[...]


def _ffn_prompt(x, g, w_up, conv_w, conv_b, w_down, n_seq, tm):
    m = x.shape[0]
    n_i = m // n_seq // tm
    tn = FFN_TN
    nj = FFN_NJ
    row = lambda b, i, j: (b * n_i + i, 0)
    return pl.pallas_call(
        _ffn_prompt_kernel,
        grid=(n_seq, n_i, nj),
        in_specs=[
            pl.BlockSpec((tm, D_MODEL), row),
            pl.BlockSpec((1, D_MODEL), lambda b, i, j: (0, 0)),
            pl.BlockSpec((D_MODEL, tn), lambda b, i, j: (0, j)),
            pl.BlockSpec((D_MODEL, tn), lambda b, i, j: (0, j + nj)),
            pl.BlockSpec((CONV_W, tn), lambda b, i, j: (0, j)),
            pl.BlockSpec((CONV_W, tn), lambda b, i, j: (0, j + nj)),
            pl.BlockSpec((1, tn), lambda b, i, j: (0, j)),
            pl.BlockSpec((1, tn), lambda b, i, j: (0, j + nj)),
            pl.BlockSpec((tn, D_MODEL), lambda b, i, j: (j, 0)),
        ],
        out_specs=[
            pl.BlockSpec((tm, D_MODEL), row),
            pl.BlockSpec((1, CONV_W - 1, tn), lambda b, i, j: (b * n_i + i, 0, j)),
            pl.BlockSpec((1, CONV_W - 1, tn), lambda b, i, j: (b * n_i + i, 0, j)),
        ],
        out_shape=[
            jax.ShapeDtypeStruct((m, D_MODEL), F32),
            jax.ShapeDtypeStruct((n_seq * n_i, CONV_W - 1, D_FF), F32),
            jax.ShapeDtypeStruct((n_seq * n_i, CONV_W - 1, D_FF), F32),
        ],
        scratch_shapes=[
            pltpu.VMEM((tm, D_MODEL), BF16),
            pltpu.VMEM((tm, D_MODEL), F32),
            pltpu.VMEM((2, nj, CARRY_ROWS, tn), F32),
            pltpu.VMEM((tm + CARRY_ROWS, tn), F32),
            pltpu.VMEM((tm + CARRY_ROWS, tn), F32),
        ],
        compiler_params=pltpu.CompilerParams(
            dimension_semantics=("arbitrary", "arbitrary", "arbitrary"), vmem_limit_bytes=VMEM_LIMIT),
        name="ffn_prompt",
    )(x, g, w_up, w_up, conv_w, conv_w, conv_b, conv_b, w_down)


def _ffn_decode_kernel(x_ref, g_ref, wug_ref, wuv_ref, cwg_ref, cwv_ref, cbg_ref, cbv_ref,
                       s0g_ref, s1g_ref, s0v_ref, s1v_ref, wd_ref,
                       o_ref, upg_ref, upv_ref, h_scr, acc_scr):
    j = pl.program_id(0)

    @pl.when(j == 0)
    def _():
        h_scr[...] = _rms(x_ref[...], g_ref[...]).astype(BF16)
        acc_scr[...] = jnp.zeros_like(acc_scr)

    h = h_scr[...]

    def conv(w_ref, cw_ref, cb_ref, s0_ref, s1_ref, up_ref):
        up = _dot(h, w_ref[...])
        up_ref[...] = up
        cw = cw_ref[...]
        out = cb_ref[...] + cw[0:1, :] * s0_ref[...]
        out = out + cw[1:2, :] * s1_ref[...]
        return out + cw[2:3, :] * up

    gate = conv(wug_ref, cwg_ref, cbg_ref, s0g_ref, s1g_ref, upg_ref)
    val = conv(wuv_ref, cwv_ref, cbv_ref, s0v_ref, s1v_ref, upv_ref)
    act = (_silu(gate) * val).astype(BF16)
    acc_scr[...] += _dot(act, wd_ref[...])

    @pl.when(j == FFN_NJ - 1)
    def _():
        o_ref[...] = x_ref[...] + acc_scr[...]


def _ffn_decode(x, g, w_up, conv_w, conv_b, s0, s1, w_down):
    m = x.shape[0]
    tn = FFN_TN
    nj = FFN_NJ
    lo = lambda j: (0, j)
    hi = lambda j: (0, j + nj)
    return pl.pallas_call(
        _ffn_decode_kernel,
        grid=(nj,),
        in_specs=[
            pl.BlockSpec((m, D_MODEL), lambda j: (0, 0)),
            pl.BlockSpec((1, D_MODEL), lambda j: (0, 0)),
            pl.BlockSpec((D_MODEL, tn), lo),
            pl.BlockSpec((D_MODEL, tn), hi),
            pl.BlockSpec((CONV_W, tn), lo),
            pl.BlockSpec((CONV_W, tn), hi),
            pl.BlockSpec((1, tn), lo),
            pl.BlockSpec((1, tn), hi),
            pl.BlockSpec((m, tn), lo),
            pl.BlockSpec((m, tn), lo),
            pl.BlockSpec((m, tn), hi),
            pl.BlockSpec((m, tn), hi),
            pl.BlockSpec((tn, D_MODEL), lambda j: (j, 0)),
        ],
        out_specs=[
            pl.BlockSpec((m, D_MODEL), lambda j: (0, 0)),
            pl.BlockSpec((m, tn), lo),
            pl.BlockSpec((m, tn), lo),
        ],
        out_shape=[
            jax.ShapeDtypeStruct((m, D_MODEL), F32),
            jax.ShapeDtypeStruct((m, D_FF), F32),
            jax.ShapeDtypeStruct((m, D_FF), F32),
        ],
        scratch_shapes=[pltpu.VMEM((m, D_MODEL), BF16), pltpu.VMEM((m, D_MODEL), F32)],
        compiler_params=pltpu.CompilerParams(dimension_semantics=("arbitrary",), vmem_limit_bytes=VMEM_LIMIT),
        name="ffn_decode",
    )(x, g, w_up, w_up, conv_w, conv_w, conv_b, conv_b, s0, s1, s0, s1, w_down)


def _mix_prompt_kernel(z_ref, lbl_ref, a_ref, hg_ref, qg_ref, kg_ref, sink_ref, lng_ref, lnb_ref, ws_ref, bst_ref,
                       mix_ref, hst_ref, klast_ref, vlast_ref,
                       st_scr, kprev, vprev, e_scr, *, layer, n_blocks):
    i = pl.program_id(1)

    @pl.when(i == 0)
    def _():
        st_scr[...] = jnp.zeros_like(st_scr)
        kprev[...] = jnp.zeros_like(kprev)
        vprev[...] = jnp.zeros_like(vprev)

    row = lax.broadcasted_iota(jnp.int32, (BLK, BLK), 0)
    col = lax.broadcasted_iota(jnp.int32, (BLK, BLK), 1)

    logf, kk = _forget_gate(lbl_ref, layer, z_ref[:, OFF_F:OFF_F + A_WIDTH])
    l_hi = logf.astype(BF16)
    l_lo = (logf - l_hi.astype(F32)).astype(BF16)
    a = a_ref[...]
    for c0 in range(0, A_WIDTH, 256):
        e_scr[:, c0:c0 + 256] = jnp.exp(_dot(a, l_hi[:, c0:c0 + 256]) + _dot(a, l_lo[:, c0:c0 + 256]))
    qq = _silu(z_ref[:, OFF_Q:OFF_Q + A_WIDTH])

    masks = [row == col]
    for k in range(N_LEVELS):
        masks.append(((row >> (k + 1)) == (col >> (k + 1))) & (((row >> k) & 1) == 1) & (((col >> k) & 1) == 0))

    hg = hg_ref[...]
    for hd in range(A_HEADS):
        hs = slice(hd * A_DK, (hd + 1) * A_DK)
        qh = qq[:, hs]
        kh = kk[:, hs]
        vb = z_ref[:, OFF_I + hd * A_DV:OFF_I + (hd + 1) * A_DV].astype(BF16)
        e_b = e_scr[0:BLK, hs]
        e_rev = e_scr[BLK:2 * BLK, hs]
        sc = jnp.where(masks[0], _dot_nt(qh.astype(BF16), kh.astype(BF16)), 0.0)
        for k in range(N_LEVELS):
            e_l = e_scr[(2 + k) * BLK:(3 + k) * BLK, hs]
            sc = sc + jnp.where(masks[k + 1], _dot_nt((qh * e_l).astype(BF16), (kh * e_l).astype(BF16)), 0.0)
        st = st_scr[hd]
        o = _dot(sc.astype(BF16), vb) + _dot_nt((qh * e_b).astype(BF16), st.astype(BF16))
        upd = lax.dot_general(vb, (kh * e_rev).astype(BF16), _TN, preferred_element_type=F32)
        st_new = st * e_b[BLK - 1:BLK, :] + upd
        st_scr[hd] = st_new
        gate = _silu(z_ref[:, OFF_G + hd * A_DV:OFF_G + (hd + 1) * A_DV])
        mix_ref[:, hs] = (_rms(o, hg) * gate).astype(mix_ref.dtype)

        @pl.when(i == n_blocks - 1)
        def _():
            hst_ref[0, hd] = st_new.T

    lane_lo = lax.broadcasted_iota(jnp.int32, (BLK, 128), 1) < B_DH
    kn = _segment_rms(z_ref[:, OFF_SK:OFF_SK + KV_WIDTH], kg_ref[...], lane_lo)
    vv = z_ref[:, OFF_SV:OFF_SV + KV_WIDTH]
    kcat = jnp.concatenate([kprev[...], kn], axis=0).astype(BF16)
    vcat = jnp.concatenate([vprev[...], vv], axis=0).astype(BF16)
    kprev[...] = kn
    vprev[...] = vv
    klast_ref[0] = kn
    vlast_ref[0] = vv

    qi = lax.broadcasted_iota(jnp.int32, (BLK, 2 * BLK), 0)
    kj = lax.broadcasted_iota(jnp.int32, (BLK, 2 * BLK), 1)
    first_key = jnp.where(i > 0, 0, BLK)
    allowed = (kj > qi) & (kj <= qi + WINDOW) & (kj >= first_key)
    sinks = sink_ref[...]
    scale = np.float32(B_DH ** -0.5)
    for c in range(B_HEADS // 2):
        kvh = (2 * c) // B_GROUP
        kv_lanes = lane_lo if kvh == 0 else jnp.logical_not(lane_lo)
        qn = _segment_rms(z_ref[:, OFF_SQ + 128 * c:OFF_SQ + 128 * (c + 1)], qg_ref[...], lane_lo) * scale
        qn_r = pltpu.roll(qn, B_DH, 1)
        outs = []
        for half in range(2):
            hd = 2 * c + half
            src = qn if half == kvh else qn_r
            qm = jnp.where(kv_lanes, src, 0.0).astype(BF16)
            s = jnp.where(allowed, _dot_nt(qm, kcat), -jnp.inf)
            sk = sinks[0:1, hd:hd + 1]
            m = jnp.maximum(jnp.max(s, axis=-1, keepdims=True), sk)
            p = jnp.exp(s - m)
            den = jnp.sum(p, axis=-1, keepdims=True) + jnp.exp(sk - m)
            ov = _dot(p.astype(BF16), vcat) / den
            outs.append(ov if half == kvh else pltpu.roll(ov, B_DH, 1))
        mix_ref[:, A_WIDTH + 128 * c:A_WIDTH + 128 * (c + 1)] = jnp.where(lane_lo, outs[0], outs[1]).astype(mix_ref.dtype)

    gl = _gelu_tanh(z_ref[:, OFF_C:OFF_C + 2 * C_WIDTH])
    u = gl[:, 0:C_WIDTH]
    vn = _layer_norm(gl[:, C_WIDTH:], lng_ref[...], lnb_ref[...])
    bst = bst_ref[...]
    for gi in range(C_GROUPS):
        gs = slice(gi * C_DG, (gi + 1) * C_DG)
        w = jnp.where(col <= row, ws_ref[gi], 0.0).astype(BF16)
        mixed = _dot(w, vn[:, gs].astype(BF16)) + bst[:, gi:gi + 1]
        mix_ref[:, A_WIDTH + B_WIDTH + gi * C_DG:A_WIDTH + B_WIDTH + (gi + 1) * C_DG] = (u[:, gs] * mixed).astype(mix_ref.dtype)


def _mix_prompt(z, lbl, hg, qg2, kg2, sinks, lng, lnb, ws, bst, n_seq, layer):
    m = z.shape[0]
    n_blocks = m // n_seq // BLK
    depth = lbl.shape[0]
    full = lambda *shape: pl.BlockSpec(shape, lambda b, i: (0,) * len(shape))
    return pl.pallas_call(
        functools.partial(_mix_prompt_kernel, layer=layer, n_blocks=n_blocks),
        grid=(n_seq, n_blocks),
        in_specs=[
            pl.BlockSpec((BLK, D_IN), lambda b, i: (b * n_blocks + i, 0)),
            full(depth, A_WIDTH),
            full(N_DECAY_ROWS, BLK),
            full(1, A_DV),
            full(1, 2 * B_DH),
            full(1, 2 * B_DH),
            full(1, B_HEADS),
            full(1, C_WIDTH),
            full(1, C_WIDTH),
            full(C_GROUPS, C_CHUNK, C_CHUNK),
            full(C_CHUNK, C_GROUPS),
        ],
        out_specs=[
            pl.BlockSpec((BLK, D_MODEL), lambda b, i: (b * n_blocks + i, 0)),
            pl.BlockSpec((1, A_HEADS, A_DK, A_DV), lambda b, i: (b, 0, 0, 0)),
            pl.BlockSpec((1, WINDOW, KV_WIDTH), lambda b, i: (b, 0, 0)),
            pl.BlockSpec((1, WINDOW, KV_WIDTH), lambda b, i: (b, 0, 0)),
        ],
        out_shape=[
            jax.ShapeDtypeStruct((m, D_MODEL), BF16),
            jax.ShapeDtypeStruct((n_seq, A_HEADS, A_DK, A_DV), F32),
            jax.ShapeDtypeStruct((n_seq, WINDOW, KV_WIDTH), F32),
            jax.ShapeDtypeStruct((n_seq, WINDOW, KV_WIDTH), F32),
        ],
        scratch_shapes=[
            pltpu.VMEM((A_HEADS, A_DV, A_DK), F32),
            pltpu.VMEM((BLK, KV_WIDTH), F32),
            pltpu.VMEM((BLK, KV_WIDTH), F32),
            pltpu.VMEM((N_DECAY_ROWS, A_WIDTH), F32),
        ],
        compiler_params=pltpu.CompilerParams(
            dimension_semantics=("arbitrary", "arbitrary"), vmem_limit_bytes=VMEM_LIMIT),
        name="mix_prompt",
    )(z, lbl, jnp.asarray(_DECAY_A, BF16), hg, qg2, kg2, sinks, lng, lnb, ws, bst)


DEC_TILE = 8


def _mix_decode_kernel(z_ref, lbl_ref, hst_ref, ck_ref, cv_ref, hg_ref, qg_ref, kg_ref, sink_ref, lng_ref, lnb_ref,
                       w0_ref, b0_ref,
                       mix_ref, hnew_ref, knew_ref, vnew_ref, vn_ref,
                       f_scr, k_scr, q_scr, o_scr, qn_scr, kn_scr, ob_scr, *, layer):
    bt = z_ref.shape[0]
    eye = lax.broadcasted_iota(jnp.int32, (A_DK, A_DK), 0) == lax.broadcasted_iota(jnp.int32, (A_DK, A_DK), 1)
    lane8 = lax.broadcasted_iota(jnp.int32, (B_HEADS, 128), 1)
    row8 = lax.broadcasted_iota(jnp.int32, (B_HEADS, 128), 0)
    lane1_lo = lax.broadcasted_iota(jnp.int32, (1, 128), 1) < B_DH
    lane_lo = lax.broadcasted_iota(jnp.int32, (bt, 128), 1) < B_DH

    logf, kk = _forget_gate(lbl_ref, layer, z_ref[:, OFF_F:OFF_F + A_WIDTH])
    f_scr[...] = jnp.exp(logf)
    k_scr[...] = kk
    q_scr[...] = _silu(z_ref[:, OFF_Q:OFF_Q + A_WIDTH])

    scale = np.float32(B_DH ** -0.5)
    for c in range(B_HEADS // 2):
        qn_scr[:, 128 * c:128 * (c + 1)] = _segment_rms(
            z_ref[:, OFF_SQ + 128 * c:OFF_SQ + 128 * (c + 1)], qg_ref[...], lane_lo) * scale
    kn_scr[...] = _segment_rms(z_ref[:, OFF_SK:OFF_SK + KV_WIDTH], kg_ref[...], lane_lo)
    sink_col = sink_ref[...]

    def column(rowvec):
        return jnp.sum(jnp.where(eye, rowvec, 0.0), axis=1, keepdims=True)

    def body(b, carry):
        for hd in range(A_HEADS):
            hs = slice(hd * A_DK, (hd + 1) * A_DK)
            f_col = column(f_scr[pl.ds(b, 1), hs])
            k_col = column(k_scr[pl.ds(b, 1), hs])
            q_col = column(q_scr[pl.ds(b, 1), hs])
            v_row = z_ref[pl.ds(b, 1), OFF_I + hd * A_DV:OFF_I + (hd + 1) * A_DV]
            s_new = hst_ref[b, hd] * f_col + k_col * v_row
            hnew_ref[b, hd] = s_new
            o_scr[pl.ds(b, 1), hs] = jnp.sum(s_new * q_col, axis=0, keepdims=True)

        kc = ck_ref[b]
        vc = cv_ref[b]
        k_row = kn_scr[pl.ds(b, 1), :]
        v_row = z_ref[pl.ds(b, 1), OFF_SV:OFF_SV + KV_WIDTH]
        qp = jnp.zeros((B_HEADS, 128), F32)
        for c in range(B_HEADS // 2):
            kvh = (2 * c) // B_GROUP
            q_row = qn_scr[pl.ds(b, 1), 128 * c:128 * (c + 1)]
            q_rolled = pltpu.roll(q_row, B_DH, 1)
            kv_lanes = lane1_lo if kvh == 0 else jnp.logical_not(lane1_lo)
            for half in range(2):
                src = jnp.where(kv_lanes, q_row if half == kvh else q_rolled, 0.0)
                qp = jnp.where(row8 == 2 * c + half, src, qp)
        s = jnp.where(lane8 >= 1, _dot_nt(qp.astype(BF16), kc.astype(BF16)), -jnp.inf)
        s_self = jnp.sum(qp * k_row, axis=1, keepdims=True)
        m = jnp.maximum(jnp.maximum(jnp.max(s, axis=1, keepdims=True), s_self), sink_col)
        p = jnp.exp(s - m)
        p_self = jnp.exp(s_self - m)
        den = jnp.sum(p, axis=1, keepdims=True) + p_self + jnp.exp(sink_col - m)
        ov = (_dot(p.astype(BF16), vc.astype(BF16)) + p_self * v_row) / den
        for c in range(B_HEADS // 2):
            kvh = (2 * c) // B_GROUP
            o_a = ov[2 * c:2 * c + 1, :]
            o_b = ov[2 * c + 1:2 * c + 2, :]
            if kvh == 0:
                o_b = pltpu.roll(o_b, B_DH, 1)
            else:
                o_a = pltpu.roll(o_a, B_DH, 1)
            ob_scr[pl.ds(b, 1), 128 * c:128 * (c + 1)] = jnp.where(lane1_lo, o_a, o_b)
        knew_ref[b, 0:WINDOW - 1, :] = ck_ref[b, 1:WINDOW, :]
        knew_ref[b, WINDOW - 1:WINDOW, :] = k_row
        vnew_ref[b, 0:WINDOW - 1, :] = cv_ref[b, 1:WINDOW, :]
        vnew_ref[b, WINDOW - 1:WINDOW, :] = v_row
        return carry

    for b in range(bt):
        body(b, 0)

    hg = hg_ref[...]
    for hd in range(A_HEADS):
        hs = slice(hd * A_DK, (hd + 1) * A_DK)
        gate = _silu(z_ref[:, OFF_G + hd * A_DV:OFF_G + (hd + 1) * A_DV])
        mix_ref[:, hs] = _rms(o_scr[:, hs], hg) * gate
    mix_ref[:, A_WIDTH:A_WIDTH + B_WIDTH] = ob_scr[...]

    gl = _gelu_tanh(z_ref[:, OFF_C:OFF_C + 2 * C_WIDTH])
    vn = _layer_norm(gl[:, C_WIDTH:], lng_ref[...], lnb_ref[...])
    vn_ref[...] = vn
    mix_ref[:, A_WIDTH + B_WIDTH:] = gl[:, 0:C_WIDTH] * (w0_ref[...] * vn + b0_ref[...])


def _mix_decode(z, lbl, hst, ck, cv, hg, qg2, kg2, sink_col, lng, lnb, w0, b0, layer):
    m = z.shape[0]
    bt = DEC_TILE
    depth = lbl.shape[0]
    full = lambda *shape: pl.BlockSpec(shape, lambda i: (0,) * len(shape))
    return pl.pallas_call(
        functools.partial(_mix_decode_kernel, layer=layer),
        grid=(m // bt,),
        in_specs=[
            pl.BlockSpec((bt, D_IN), lambda i: (i, 0)),
            full(depth, A_WIDTH),
            pl.BlockSpec((bt, A_HEADS, A_DK, A_DV), lambda i: (i, 0, 0, 0)),
            pl.BlockSpec((bt, WINDOW, KV_WIDTH), lambda i: (i, 0, 0)),
            pl.BlockSpec((bt, WINDOW, KV_WIDTH), lambda i: (i, 0, 0)),
            full(1, A_DV),
            full(1, 2 * B_DH),
            full(1, 2 * B_DH),
            full(B_HEADS, 1),
            full(1, C_WIDTH),
            full(1, C_WIDTH),
            full(1, C_WIDTH),
            full(1, C_WIDTH),
        ],
        out_specs=[
            pl.BlockSpec((bt, D_MODEL), lambda i: (i, 0)),
            pl.BlockSpec((bt, A_HEADS, A_DK, A_DV), lambda i: (i, 0, 0, 0)),
            pl.BlockSpec((bt, WINDOW, KV_WIDTH), lambda i: (i, 0, 0)),
            pl.BlockSpec((bt, WINDOW, KV_WIDTH), lambda i: (i, 0, 0)),
            pl.BlockSpec((bt, C_WIDTH), lambda i: (i, 0)),
        ],
        out_shape=[
            jax.ShapeDtypeStruct((m, D_MODEL), F32),
            jax.ShapeDtypeStruct((m, A_HEADS, A_DK, A_DV), F32),
            jax.ShapeDtypeStruct((m, WINDOW, KV_WIDTH), F32),
            jax.ShapeDtypeStruct((m, WINDOW, KV_WIDTH), F32),
            jax.ShapeDtypeStruct((m, C_WIDTH), F32),
        ],
        scratch_shapes=[
            pltpu.VMEM((bt, A_WIDTH), F32),
            pltpu.VMEM((bt, A_WIDTH), F32),
            pltpu.VMEM((bt, A_WIDTH), F32),
            pltpu.VMEM((bt, A_WIDTH), F32),
            pltpu.VMEM((bt, B_WIDTH), F32),
            pltpu.VMEM((bt, KV_WIDTH), F32),
            pltpu.VMEM((bt, B_WIDTH), F32),
        ],
        compiler_params=pltpu.CompilerParams(dimension_semantics=("parallel",), vmem_limit_bytes=VMEM_LIMIT),
        name="mix_decode",
    )(z, lbl, hst, ck, cv, hg, qg2, kg2, sink_col, lng, lnb, w0, b0)


def kernel(x_prompt, x_sample, state_hgrn, cache_swa_k, cache_swa_v, state_ffn_conv, p_prompt, p_sample, norm1_g, w_in, hgrn_lb_logits, hgrn_norm_g, q_norm_g, k_norm_g, swa_sinks, gmlp_ln_g, gmlp_ln_b, gmlp_ws, gmlp_bs, w_out, norm2_g, w_up, conv_w, conv_b, w_down, ple_norm_g, w_ple_gate, w_ple_proj):
    depth = w_in.shape[0]
    n_seq, seq, _ = x_prompt.shape
    n_dec = x_sample.shape[0]
    xp = x_prompt.reshape(n_seq * seq, D_MODEL)
    xs = x_sample.reshape(n_dec, D_MODEL)
    hp, hs, kp, vp, ks, vs, gs, cp, cs = [], [], [], [], [], [], [], [], []
    for l in range(depth):
        wi = w_in[l].astype(BF16)
        wo = w_out[l].astype(BF16)
        wu = w_up[l].astype(BF16)
        wd = w_down[l].astype(BF16)
        wg = w_ple_gate[l].astype(BF16)
        wp = w_ple_proj[l].astype(BF16)
        n1 = norm1_g[l].reshape(1, D_MODEL)
        n2 = norm2_g[l].reshape(1, D_MODEL)
        n3 = ple_norm_g[l].reshape(1, D_MODEL)
        hg = hgrn_norm_g[l].reshape(1, A_DV)
        qg2 = jnp.tile(q_norm_g[l], 2).reshape(1, 2 * B_DH)
        kg2 = jnp.tile(k_norm_g[l], 2).reshape(1, 2 * B_DH)
        lng = gmlp_ln_g[l].reshape(1, C_WIDTH)
        lnb = gmlp_ln_b[l].reshape(1, C_WIDTH)
        cb = conv_b[l].reshape(1, 2 * D_FF)

        z = _in_proj(xp, n1, wi, 256)
        mix, h_new, k_new, v_new = _mix_prompt(
            z, hgrn_lb_logits, hg, qg2, kg2, swa_sinks[l].reshape(1, B_HEADS), lng, lnb,
            gmlp_ws[l], gmlp_bs[l].T, n_seq, l)
        x1 = _out_proj(mix, xp, wo, 512)
        x2, cg, cv = _ffn_prompt(x1, n2, wu, conv_w[l], cb, wd, n_seq, 512)
        xp = _ple(x2, p_prompt[l].reshape(n_seq * seq, PLE_DIM), n3, wg, wp, 512)
        hp.append(h_new)
        kp.append(k_new.reshape(n_seq, WINDOW, B_KV_HEADS, B_DH))
        vp.append(v_new.reshape(n_seq, WINDOW, B_KV_HEADS, B_DH))
        last = lambda t: t.reshape(n_seq, -1, CONV_W - 1, D_FF)[:, -1]
        cp.append(jnp.concatenate([last(cg), last(cv)], axis=-1))

        zs = _in_proj(xs, n1, wi, n_dec)
        w0 = jnp.repeat(gmlp_ws[l][:, 0, 0], C_DG).reshape(1, C_WIDTH)
        b0 = jnp.repeat(gmlp_bs[l][:, 0], C_DG).reshape(1, C_WIDTH)
        mix_s, h_new, k_new, v_new, vn = _mix_decode(
            zs, hgrn_lb_logits, state_hgrn[l],
            cache_swa_k[l].reshape(n_dec, WINDOW, KV_WIDTH), cache_swa_v[l].reshape(n_dec, WINDOW, KV_WIDTH),
            hg, qg2, kg2, swa_sinks[l].reshape(B_HEADS, 1), lng, lnb, w0, b0, l)
        x1s = _out_proj(mix_s, xs, wo, n_dec)
        st = state_ffn_conv[l]
        x2s, upg, upv = _ffn_decode(x1s, n2, wu, conv_w[l], cb, st[:, 0, :], st[:, 1, :], wd)
        xs = _ple(x2s, p_sample[l].reshape(n_dec, PLE_DIM), n3, wg, wp, n_dec)
        hs.append(h_new)
        ks.append(k_new.reshape(n_dec, WINDOW, B_KV_HEADS, B_DH))
        vs.append(v_new.reshape(n_dec, WINDOW, B_KV_HEADS, B_DH))
        gs.append(vn.reshape(n_dec, 1, C_GROUPS, C_DG))
        cs.append(jnp.stack([st[:, 1, :], jnp.concatenate([upg, upv], axis=-1)], axis=1))

    return (xp.reshape(n_seq, seq, D_MODEL), xs.reshape(n_dec, 1, D_MODEL),
            jnp.stack(hp), jnp.stack(hs), jnp.stack(kp), jnp.stack(vp), jnp.stack(ks), jnp.stack(vs),
            jnp.stack(gs), jnp.stack(cp), jnp.stack(cs))
```

```python
import functools

import numpy as np
import jax
import jax.numpy as jnp
from jax import lax
from jax.experimental import pallas as pl
from jax.experimental.pallas import tpu as pltpu

D_MODEL = 2048
PLE_DIM = 256
A_DK = 128
A_DV = 128
A_HEADS = 8
A_WIDTH = A_HEADS * A_DV
B_DH = 64
B_HEADS = 8
B_KV_HEADS = 2
B_GROUP = B_HEADS // B_KV_HEADS
B_WIDTH = B_HEADS * B_DH
WINDOW = 128
C_DG = 128
C_GROUPS = 4
C_WIDTH = C_GROUPS * C_DG
C_CHUNK = 128
D_FF = 5632
CONV_W = 3
EPS = 1e-6
D_IN = 4 * A_WIDTH + B_WIDTH + 2 * B_KV_HEADS * B_DH + 2 * C_WIDTH
KV_WIDTH = B_KV_HEADS * B_DH
OFF_Q, OFF_F, OFF_I, OFF_G = 0, A_WIDTH, 2 * A_WIDTH, 3 * A_WIDTH
OFF_SQ = 4 * A_WIDTH
OFF_SK = OFF_SQ + B_WIDTH
OFF_SV = OFF_SK + KV_WIDTH
OFF_C = OFF_SV + KV_WIDTH

BLK = 128
N_LEVELS = 7
VMEM_LIMIT = 56 * 1024 * 1024

F32 = jnp.float32
BF16 = jnp.bfloat16

_NT = (((1,), (1,)), ((), ()))
_TN = (((0,), (0,)), ((), ()))


def _dot(a, b):
    return jnp.dot(a, b, preferred_element_type=F32)


def _dot_nt(a, b):
    return lax.dot_general(a, b, _NT, preferred_element_type=F32)


_TRI = np.tril(np.ones((BLK, BLK), np.float32))
LOG2E = np.float32(1.4426950408889634)


def _sigmoid(x):
    return jax.nn.sigmoid(x)


def _silu(x):
    return x * jax.nn.sigmoid(x)


def _logaddexp(a, b):
    return jnp.maximum(a, b) + jnp.log(1.0 + jnp.exp(-jnp.abs(a - b)))


def _gelu_tanh(x):
    c = np.float32(np.sqrt(2.0 / np.pi))
    return 0.5 * x * (1.0 + jnp.tanh(c * (x + np.float32(0.044715) * (x * x * x))))


def _rms(x, g):
    return x * lax.rsqrt(jnp.mean(x * x, axis=-1, keepdims=True) + EPS) * g


def _forget_gate(lbl_ref, layer, zf):
    lg = lbl_ref[...]
    e = jnp.exp(lg - jnp.max(lg, axis=0, keepdims=True))
    tot = jnp.sum(e, axis=0, keepdims=True)
    lb = jnp.zeros_like(tot)
    for j in range(1, layer + 1):
        lb = lb + e[j:j + 1, :] / tot
    t = jnp.exp(-jnp.abs(zf))
    u = 1.0 + t
    log_sig = jnp.minimum(zf, 0.0) - jnp.log(u)
    sig_neg = jnp.where(zf >= 0.0, t, 1.0) / u
    logf = _logaddexp(jnp.log(lb), jnp.log1p(-lb) + log_sig)
    return logf, (1.0 - lb) * sig_neg


def _segment_rms(x, g2, lane_lo):
    x2 = x * x
    s_lo = jnp.sum(jnp.where(lane_lo, x2, 0.0), axis=-1, keepdims=True)
    s_hi = jnp.sum(jnp.where(lane_lo, 0.0, x2), axis=-1, keepdims=True)
    inv = jnp.where(lane_lo, lax.rsqrt(s_lo / B_DH + EPS), lax.rsqrt(s_hi / B_DH + EPS))
    return x * inv * g2


def _layer_norm(x, g, b):
    xc = x - jnp.mean(x, axis=-1, keepdims=True)
    return xc * lax.rsqrt(jnp.mean(xc * xc, axis=-1, keepdims=True) + EPS) * g + b


def _in_proj_kernel(x_ref, g_ref, w_ref, o_ref):
    h = _rms(x_ref[...], g_ref[...]).astype(BF16)
    n = o_ref.shape[1]
    step = 512
    for c0 in range(0, n, step):
        c1 = min(c0 + step, n)
        o_ref[:, c0:c1] = _dot(h, w_ref[:, c0:c1])


def _in_proj(x, g, w, tm):
    m = x.shape[0]
    n = w.shape[1]
    return pl.pallas_call(
        _in_proj_kernel,
        grid=(m // tm,),
        in_specs=[
            pl.BlockSpec((tm, D_MODEL), lambda i: (i, 0)),
            pl.BlockSpec((1, D_MODEL), lambda i: (0, 0)),
            pl.BlockSpec((D_MODEL, n), lambda i: (0, 0), pipeline_mode=pl.Buffered(1)),
        ],
        out_specs=pl.BlockSpec((tm, n), lambda i: (i, 0)),
        out_shape=jax.ShapeDtypeStruct((m, n), F32),
        compiler_params=pltpu.CompilerParams(dimension_semantics=("parallel",), vmem_limit_bytes=VMEM_LIMIT),
        name="in_proj",
    )(x, g, w)


def _out_proj_kernel(mix_ref, x_ref, w_ref, o_ref):
    o_ref[...] = x_ref[...] + _dot(mix_ref[...].astype(BF16), w_ref[...])


def _out_proj(mix, x, w, tm):
    m = x.shape[0]
    return pl.pallas_call(
        _out_proj_kernel,
        grid=(m // tm,),
        in_specs=[
            pl.BlockSpec((tm, D_MODEL), lambda i: (i, 0)),
            pl.BlockSpec((tm, D_MODEL), lambda i: (i, 0)),
            pl.BlockSpec((D_MODEL, D_MODEL), lambda i: (0, 0), pipeline_mode=pl.Buffered(1)),
        ],
        out_specs=pl.BlockSpec((tm, D_MODEL), lambda i: (i, 0)),
        out_shape=jax.ShapeDtypeStruct((m, D_MODEL), F32),
        compiler_params=pltpu.CompilerParams(dimension_semantics=("parallel",), vmem_limit_bytes=VMEM_LIMIT),
        name="out_proj",
    )(mix, x, w)


def _ple_kernel(x_ref, p_ref, g_ref, wg_ref, wp_ref, o_ref):
    x = x_ref[...]
    h = _rms(x, g_ref[...]).astype(BF16)
    gate = _sigmoid(_dot(h, wg_ref[...]))
    o_ref[...] = x + gate * _dot(p_ref[...].astype(BF16), wp_ref[...])


def _ple(x, p, g, wg, wp, tm):
    m = x.shape[0]
    return pl.pallas_call(
        _ple_kernel,
        grid=(m // tm,),
        in_specs=[
            pl.BlockSpec((tm, D_MODEL), lambda i: (i, 0)),
            pl.BlockSpec((tm, PLE_DIM), lambda i: (i, 0)),
            pl.BlockSpec((1, D_MODEL), lambda i: (0, 0)),
            pl.BlockSpec((D_MODEL, D_MODEL), lambda i: (0, 0), pipeline_mode=pl.Buffered(1)),
            pl.BlockSpec((PLE_DIM, D_MODEL), lambda i: (0, 0), pipeline_mode=pl.Buffered(1)),
        ],
        out_specs=pl.BlockSpec((tm, D_MODEL), lambda i: (i, 0)),
        out_shape=jax.ShapeDtypeStruct((m, D_MODEL), F32),
        compiler_params=pltpu.CompilerParams(dimension_semantics=("parallel",), vmem_limit_bytes=VMEM_LIMIT),
        name="ple",
    )(x, p, g, wg, wp)


FFN_TN = 512
FFN_NJ = D_FF // FFN_TN
CARRY_ROWS = 8


def _ffn_prompt_kernel(x_ref, g_ref, wug_ref, wuv_ref, cwg_ref, cwv_ref, cbg_ref, cbv_ref, wd_ref,
                       o_ref, cg_ref, cv_ref, h_scr, acc_scr, carry_scr, bufg, bufv):
    i = pl.program_id(1)
    j = pl.program_id(2)
    tm = x_ref.shape[0]

    @pl.when(j == 0)
    def _():
        h_scr[...] = _rms(x_ref[...], g_ref[...]).astype(BF16)
        acc_scr[...] = jnp.zeros_like(acc_scr)

    @pl.when(i == 0)
    def _():
        carry_scr[0, j] = jnp.zeros((CARRY_ROWS, FFN_TN), F32)
        carry_scr[1, j] = jnp.zeros((CARRY_ROWS, FFN_TN), F32)

    h = h_scr[...]

    def conv(part, w_ref, cw_ref, cb_ref, buf, last_ref):
        up = _dot(h, w_ref[...])
        buf[0:CARRY_ROWS, :] = carry_scr[part, j]
        buf[CARRY_ROWS:, :] = up
        cw = cw_ref[...]
        out = cb_ref[...] + cw[0:1, :] * buf[CARRY_ROWS - 2:CARRY_ROWS - 2 + tm, :]
        out = out + cw[1:2, :] * buf[CARRY_ROWS - 1:CARRY_ROWS - 1 + tm, :]
        out = out + cw[2:3, :] * up
        carry_scr[part, j] = buf[tm:tm + CARRY_ROWS, :]
        last_ref[0] = up[tm - (CONV_W - 1):, :]
        return out

    gate = conv(0, wug_ref, cwg_ref, cbg_ref, bufg, cg_ref)
    val = conv(1, wuv_ref, cwv_ref, cbv_ref, bufv, cv_ref)
    act = (_silu(gate) * val).astype(BF16)
    acc_scr[...] += _dot(act, wd_ref[...])

    @pl.when(j == FFN_NJ - 1)
    def _():
        o_ref[...] = x_ref[...] + acc_scr[...]


def _ffn_prompt(x, g, w_up, conv_w, conv_b, w_down, n_seq, tm):
    m = x.shape[0]
    n_i = m // n_seq // tm
    tn = FFN_TN
    nj = FFN_NJ
    row = lambda b, i, j: (b * n_i + i, 0)
    return pl.pallas_call(
        _ffn_prompt_kernel,
        grid=(n_seq, n_i, nj),
        in_specs=[
            pl.BlockSpec((tm, D_MODEL), row),
            pl.BlockSpec((1, D_MODEL), lambda b, i, j: (0, 0)),
            pl.BlockSpec((D_MODEL, tn), lambda b, i, j: (0, j)),
            pl.BlockSpec((D_MODEL, tn), lambda b, i, j: (0, j + nj)),
            pl.BlockSpec((CONV_W, tn), lambda b, i, j: (0, j)),
            pl.BlockSpec((CONV_W, tn), lambda b, i, j: (0, j + nj)),
            pl.BlockSpec((1, tn), lambda b, i, j: (0, j)),
            pl.BlockSpec((1, tn), lambda b, i, j: (0, j + nj)),
            pl.BlockSpec((tn, D_MODEL), lambda b, i, j: (j, 0)),
        ],
        out_specs=[
            pl.BlockSpec((tm, D_MODEL), row),
            pl.BlockSpec((1, CONV_W - 1, tn), lambda b, i, j: (b * n_i + i, 0, j)),
            pl.BlockSpec((1, CONV_W - 1, tn), lambda b, i, j: (b * n_i + i, 0, j)),
        ],
        out_shape=[
            jax.ShapeDtypeStruct((m, D_MODEL), F32),
            jax.ShapeDtypeStruct((n_seq * n_i, CONV_W - 1, D_FF), F32),
            jax.ShapeDtypeStruct((n_seq * n_i, CONV_W - 1, D_FF), F32),
        ],
        scratch_shapes=[
            pltpu.VMEM((tm, D_MODEL), BF16),
            pltpu.VMEM((tm, D_MODEL), F32),
            pltpu.VMEM((2, nj, CARRY_ROWS, tn), F32),
            pltpu.VMEM((tm + CARRY_ROWS, tn), F32),
            pltpu.VMEM((tm + CARRY_ROWS, tn), F32),
        ],
        compiler_params=pltpu.CompilerParams(
            dimension_semantics=("arbitrary", "arbitrary", "arbitrary"), vmem_limit_bytes=VMEM_LIMIT),
        name="ffn_prompt",
    )(x, g, w_up, w_up, conv_w, conv_w, conv_b, conv_b, w_down)


def _ffn_decode_kernel(x_ref, g_ref, wug_ref, wuv_ref, cwg_ref, cwv_ref, cbg_ref, cbv_ref,
                       s0g_ref, s1g_ref, s0v_ref, s1v_ref, wd_ref,
                       o_ref, upg_ref, upv_ref, h_scr, acc_scr):
    j = pl.program_id(0)

    @pl.when(j == 0)
    def _():
        h_scr[...] = _rms(x_ref[...], g_ref[...]).astype(BF16)
        acc_scr[...] = jnp.zeros_like(acc_scr)

    h = h_scr[...]

    def conv(w_ref, cw_ref, cb_ref, s0_ref, s1_ref, up_ref):
        up = _dot(h, w_ref[...])
        up_ref[...] = up
        cw = cw_ref[...]
        out = cb_ref[...] + cw[0:1, :] * s0_ref[...]
        out = out + cw[1:2, :] * s1_ref[...]
        return out + cw[2:3, :] * up

    gate = conv(wug_ref, cwg_ref, cbg_ref, s0g_ref, s1g_ref, upg_ref)
    val = conv(wuv_ref, cwv_ref, cbv_ref, s0v_ref, s1v_ref, upv_ref)
    act = (_silu(gate) * val).astype(BF16)
    acc_scr[...] += _dot(act, wd_ref[...])

    @pl.when(j == FFN_NJ - 1)
    def _():
        o_ref[...] = x_ref[...] + acc_scr[...]


def _ffn_decode(x, g, w_up, conv_w, conv_b, s0, s1, w_down):
    m = x.shape[0]
    tn = FFN_TN
    nj = FFN_NJ
    lo = lambda j: (0, j)
    hi = lambda j: (0, j + nj)
    return pl.pallas_call(
        _ffn_decode_kernel,
        grid=(nj,),
        in_specs=[
            pl.BlockSpec((m, D_MODEL), lambda j: (0, 0)),
            pl.BlockSpec((1, D_MODEL), lambda j: (0, 0)),
            pl.BlockSpec((D_MODEL, tn), lo),
            pl.BlockSpec((D_MODEL, tn), hi),
            pl.BlockSpec((CONV_W, tn), lo),
            pl.BlockSpec((CONV_W, tn), hi),
            pl.BlockSpec((1, tn), lo),
            pl.BlockSpec((1, tn), hi),
            pl.BlockSpec((m, tn), lo),
            pl.BlockSpec((m, tn), lo),
            pl.BlockSpec((m, tn), hi),
            pl.BlockSpec((m, tn), hi),
            pl.BlockSpec((tn, D_MODEL), lambda j: (j, 0)),
        ],
        out_specs=[
            pl.BlockSpec((m, D_MODEL), lambda j: (0, 0)),
            pl.BlockSpec((m, tn), lo),
            pl.BlockSpec((m, tn), lo),
        ],
        out_shape=[
            jax.ShapeDtypeStruct((m, D_MODEL), F32),
            jax.ShapeDtypeStruct((m, D_FF), F32),
            jax.ShapeDtypeStruct((m, D_FF), F32),
        ],
        scratch_shapes=[pltpu.VMEM((m, D_MODEL), BF16), pltpu.VMEM((m, D_MODEL), F32)],
        compiler_params=pltpu.CompilerParams(dimension_semantics=("arbitrary",), vmem_limit_bytes=VMEM_LIMIT),
        name="ffn_decode",
    )(x, g, w_up, w_up, conv_w, conv_w, conv_b, conv_b, s0, s1, s0, s1, w_down)


def _mix_prompt_kernel(z_ref, lbl_ref, tri_ref, hg_ref, qg_ref, kg_ref, sink_ref, lng_ref, lnb_ref, ws_ref, bst_ref,
                       mix_ref, hst_ref, klast_ref, vlast_ref,
                       st_scr, kprev, vprev, q_scr, k_scr, b_scr, *, layer, n_blocks):
    i = pl.program_id(1)

    @pl.when(i == 0)
    def _():
        st_scr[...] = jnp.zeros_like(st_scr)
        kprev[...] = jnp.zeros_like(kprev)
        vprev[...] = jnp.zeros_like(vprev)

    row = lax.broadcasted_iota(jnp.int32, (BLK, BLK), 0)
    col = lax.broadcasted_iota(jnp.int32, (BLK, BLK), 1)

    logf, kk = _forget_gate(lbl_ref, layer, z_ref[:, OFF_F:OFF_F + A_WIDTH])
    lf2 = logf * LOG2E
    l_hi = lf2.astype(BF16)
    l_lo = (lf2 - l_hi.astype(F32)).astype(BF16)
    tri = tri_ref[...]
    for c0 in range(0, A_WIDTH, 256):
        b_scr[:, c0:c0 + 256] = _dot(tri, l_hi[:, c0:c0 + 256]) + _dot(tri, l_lo[:, c0:c0 + 256])
    k_scr[...] = kk
    q_scr[...] = _silu(z_ref[:, OFF_Q:OFF_Q + A_WIDTH])

    split = jnp.where(row > col, row ^ col, 0)
    sub = lax.broadcasted_iota(jnp.int32, (BLK // 8, 8, A_DK), 1)
    hg = hg_ref[...]

    def level_factor(b, k):
        n = 1 << k
        if 2 * n >= 8:
            bk = b.reshape(BLK // (2 * n), 2 * n, A_DK)
            return jnp.exp2(-jnp.abs(bk - bk[:, n - 1:n, :])).reshape(BLK, A_DK)
        b3 = b.reshape(BLK // 8, 8, A_DK)
        if k == 0:
            r = jnp.where((sub & 1) == 1, pltpu.roll(b3, 1, 1), b3)
        else:
            m4 = sub & 3
            r = jnp.where(m4 == 0, pltpu.roll(b3, 7, 1),
                          jnp.where(m4 == 1, b3, jnp.where(m4 == 2, pltpu.roll(b3, 1, 1), pltpu.roll(b3, 2, 1))))
        return jnp.exp2(-jnp.abs(b3 - r)).reshape(BLK, A_DK)

    def head(hd):
        off = pl.multiple_of(hd * A_DK, A_DK)
        hs = pl.ds(off, A_DK)
        b = b_scr[:, hs]
        qh = q_scr[:, hs]
        kh = k_scr[:, hs]
        vb = z_ref[:, pl.ds(OFF_I + off, A_DV)].astype(BF16)
        sc = jnp.where(row == col, _dot_nt(qh.astype(BF16), kh.astype(BF16)), 0.0)
        for k in range(N_LEVELS):
            e_l = level_factor(b, k)
            sc = jnp.where((split >> k) == 1, _dot_nt((qh * e_l).astype(BF16), (kh * e_l).astype(BF16)), sc)
        st = st_scr[hd]
        b_last = b[BLK - 1:BLK, :]
        o = _dot(sc.astype(BF16), vb) + _dot_nt((qh * jnp.exp2(b)).astype(BF16), st.astype(BF16))
        upd = lax.dot_general(vb, (kh * jnp.exp2(b_last - b)).astype(BF16), _TN, preferred_element_type=F32)
        st_scr[hd] = st * jnp.exp2(b_last) + upd
        gate = _silu(z_ref[:, pl.ds(OFF_G + off, A_DV)])
        mix_ref[:, hs] = (_rms(o, hg) * gate).astype(mix_ref.dtype)

    def two_heads(it, carry):
        head(2 * it)
        head(2 * it + 1)
        return carry

    lax.fori_loop(0, A_HEADS // 2, two_heads, 0)

    lane_lo = lax.broadcasted_iota(jnp.int32, (BLK, 128), 1) < B_DH
    kn = _segment_rms(z_ref[:, OFF_SK:OFF_SK + KV_WIDTH], kg_ref[...], lane_lo)
    vv = z_ref[:, OFF_SV:OFF_SV + KV_WIDTH]
    kcat = jnp.concatenate([kprev[...], kn], axis=0).astype(BF16)
    vcat_t = jnp.concatenate([vprev[...].T, vv.T], axis=1).astype(BF16)
    kprev[...] = kn
    vprev[...] = vv
    klast_ref[0] = kn
    vlast_ref[0] = vv

    kj = lax.broadcasted_iota(jnp.int32, (2 * BLK, B_GROUP * BLK), 0)
    qi = lax.broadcasted_iota(jnp.int32, (2 * BLK, B_GROUP * BLK), 1) & (BLK - 1)
    first_key = jnp.where(i > 0, 0, BLK)
    allowed = (kj > qi) & (kj <= qi + WINDOW) & (kj >= first_key)
    scale = np.float32(B_DH ** -0.5)
    for g in range(B_KV_HEADS):
        kv_lanes = lane_lo if g == 0 else jnp.logical_not(lane_lo)
        qs = []
        for c in range(2 * g, 2 * g + 2):
            qn = _segment_rms(z_ref[:, OFF_SQ + 128 * c:OFF_SQ + 128 * (c + 1)], qg_ref[...], lane_lo) * scale
            qn_r = pltpu.roll(qn, B_DH, 1)
            for half in range(2):
                qs.append(jnp.where(kv_lanes, qn if half == g else qn_r, 0.0).astype(BF16))
        q4 = jnp.concatenate(qs, axis=0)
        s = jnp.where(allowed, _dot_nt(kcat, q4), -jnp.inf)
        sk = sink_ref[:, B_GROUP * BLK * g:B_GROUP * BLK * (g + 1)]
        m = jnp.maximum(jnp.max(s, axis=0, keepdims=True), sk)
        p = jnp.exp(s - m)
        inv = 1.0 / (jnp.sum(p, axis=0, keepdims=True) + jnp.exp(sk - m))
        ot = _dot(vcat_t[B_DH * g:B_DH * (g + 1), :], p.astype(BF16)) * inv
        for cc in range(2):
            pair = jnp.concatenate([ot[:, 2 * cc * BLK:(2 * cc + 1) * BLK], ot[:, (2 * cc + 1) * BLK:(2 * cc + 2) * BLK]], axis=0)
            c = 2 * g + cc
            mix_ref[:, A_WIDTH + 128 * c:A_WIDTH + 128 * (c + 1)] = pair.T.astype(mix_ref.dtype)

    gl = _gelu_tanh(z_ref[:, OFF_C:OFF_C + 2 * C_WIDTH])
    u = gl[:, 0:C_WIDTH]
    vn = _layer_norm(gl[:, C_WIDTH:], lng_ref[...], lnb_ref[...])
    bst = bst_ref[...]
    for gi in range(C_GROUPS):
        gs = slice(gi * C_DG, (gi + 1) * C_DG)
        w = jnp.where(col <= row, ws_ref[gi], 0.0).astype(BF16)
        mixed = _dot(w, vn[:, gs].astype(BF16)) + bst[:, gi:gi + 1]
        mix_ref[:, A_WIDTH + B_WIDTH + gi * C_DG:A_WIDTH + B_WIDTH + (gi + 1) * C_DG] = (u[:, gs] * mixed).astype(mix_ref.dtype)

    @pl.when(i == n_blocks - 1)
    def _():
        for hd in range(A_HEADS):
            hst_ref[0, hd] = st_scr[hd].T


def _mix_prompt(z, lbl, hg, qg2, kg2, sinks, lng, lnb, ws, bst, n_seq, layer):
    m = z.shape[0]
    n_blocks = m // n_seq // BLK
    depth = lbl.shape[0]
    full = lambda *shape: pl.BlockSpec(shape, lambda b, i: (0,) * len(shape))
    return pl.pallas_call(
        functools.partial(_mix_prompt_kernel, layer=layer, n_blocks=n_blocks),
        grid=(n_seq, n_blocks),
        in_specs=[
            pl.BlockSpec((BLK, D_IN), lambda b, i: (b * n_blocks + i, 0)),
            full(depth, A_WIDTH),
            full(BLK, BLK),
            full(1, A_DV),
            full(1, 2 * B_DH),
            full(1, 2 * B_DH),
            full(1, B_HEADS * BLK),
            full(1, C_WIDTH),
            full(1, C_WIDTH),
            full(C_GROUPS, C_CHUNK, C_CHUNK),
            full(C_CHUNK, C_GROUPS),
        ],
        out_specs=[
            pl.BlockSpec((BLK, D_MODEL), lambda b, i: (b * n_blocks + i, 0)),
            pl.BlockSpec((1, A_HEADS, A_DK, A_DV), lambda b, i: (b, 0, 0, 0)),
            pl.BlockSpec((1, WINDOW, KV_WIDTH), lambda b, i: (b, 0, 0)),
            pl.BlockSpec((1, WINDOW, KV_WIDTH), lambda b, i: (b, 0, 0)),
        ],
        out_shape=[
            jax.ShapeDtypeStruct((m, D_MODEL), BF16),
            jax.ShapeDtypeStruct((n_seq, A_HEADS, A_DK, A_DV), F32),
            jax.ShapeDtypeStruct((n_seq, WINDOW, KV_WIDTH), F32),
            jax.ShapeDtypeStruct((n_seq, WINDOW, KV_WIDTH), F32),
        ],
        scratch_shapes=[
            pltpu.VMEM((A_HEADS, A_DV, A_DK), F32),
            pltpu.VMEM((BLK, KV_WIDTH), F32),
            pltpu.VMEM((BLK, KV_WIDTH), F32),
            pltpu.VMEM((BLK, A_WIDTH), F32),
            pltpu.VMEM((BLK, A_WIDTH), F32),
            pltpu.VMEM((BLK, A_WIDTH), F32),
        ],
        compiler_params=pltpu.CompilerParams(
            dimension_semantics=("arbitrary", "arbitrary"), vmem_limit_bytes=VMEM_LIMIT),
        name="mix_prompt",
    )(z, lbl, jnp.asarray(_TRI, BF16), hg, qg2, kg2, sinks, lng, lnb, ws, bst)


DEC_TILE = 8


def _mix_decode_kernel(z_ref, lbl_ref, hst_ref, ck_ref, cv_ref, hg_ref, qg_ref, kg_ref, sink_ref, lng_ref, lnb_ref,
                       w0_ref, b0_ref,
                       mix_ref, hnew_ref, knew_ref, vnew_ref, vn_ref,
                       f_scr, k_scr, q_scr, o_scr, qn_scr, kn_scr, ob_scr, *, layer):
    bt = z_ref.shape[0]
    eye = lax.broadcasted_iota(jnp.int32, (A_DK, A_DK), 0) == lax.broadcasted_iota(jnp.int32, (A_DK, A_DK), 1)
    lane8 = lax.broadcasted_iota(jnp.int32, (B_HEADS, 128), 1)
    row8 = lax.broadcasted_iota(jnp.int32, (B_HEADS, 128), 0)
    lane1_lo = lax.broadcasted_iota(jnp.int32, (1, 128), 1) < B_DH
    lane_lo = lax.broadcasted_iota(jnp.int32, (bt, 128), 1) < B_DH

    logf, kk = _forget_gate(lbl_ref, layer, z_ref[:, OFF_F:OFF_F + A_WIDTH])
    f_scr[...] = jnp.exp(logf)
    k_scr[...] = kk
    q_scr[...] = _silu(z_ref[:, OFF_Q:OFF_Q + A_WIDTH])

    scale = np.float32(B_DH ** -0.5)
    for c in range(B_HEADS // 2):
        qn_scr[:, 128 * c:128 * (c + 1)] = _segment_rms(
            z_ref[:, OFF_SQ + 128 * c:OFF_SQ + 128 * (c + 1)], qg_ref[...], lane_lo) * scale
    kn_scr[...] = _segment_rms(z_ref[:, OFF_SK:OFF_SK + KV_WIDTH], kg_ref[...], lane_lo)
    sink_col = sink_ref[...]

    def column(rowvec):
        return jnp.sum(jnp.where(eye, rowvec, 0.0), axis=1, keepdims=True)

    def body(b, carry):
        for hd in range(A_HEADS):
            hs = slice(hd * A_DK, (hd + 1) * A_DK)
            f_col = column(f_scr[pl.ds(b, 1), hs])
            k_col = column(k_scr[pl.ds(b, 1), hs])
            q_col = column(q_scr[pl.ds(b, 1), hs])
            v_row = z_ref[pl.ds(b, 1), OFF_I + hd * A_DV:OFF_I + (hd + 1) * A_DV]
            s_new = hst_ref[b, hd] * f_col + k_col * v_row
            hnew_ref[b, hd] = s_new
            o_scr[pl.ds(b, 1), hs] = jnp.sum(s_new * q_col, axis=0, keepdims=True)

        kc = ck_ref[b]
        vc = cv_ref[b]
        k_row = kn_scr[pl.ds(b, 1), :]
        v_row = z_ref[pl.ds(b, 1), OFF_SV:OFF_SV + KV_WIDTH]
        qp = jnp.zeros((B_HEADS, 128), F32)
        for c in range(B_HEADS // 2):
            kvh = (2 * c) // B_GROUP
            q_row = qn_scr[pl.ds(b, 1), 128 * c:128 * (c + 1)]
            q_rolled = pltpu.roll(q_row, B_DH, 1)
            kv_lanes = lane1_lo if kvh == 0 else jnp.logical_not(lane1_lo)
            for half in range(2):
                src = jnp.where(kv_lanes, q_row if half == kvh else q_rolled, 0.0)
                qp = jnp.where(row8 == 2 * c + half, src, qp)
        s = jnp.where(lane8 >= 1, _dot_nt(qp.astype(BF16), kc.astype(BF16)), -jnp.inf)
        s_self = jnp.sum(qp * k_row, axis=1, keepdims=True)
        m = jnp.maximum(jnp.maximum(jnp.max(s, axis=1, keepdims=True), s_self), sink_col)
        p = jnp.exp(s - m)
        p_self = jnp.exp(s_self - m)
        den = jnp.sum(p, axis=1, keepdims=True) + p_self + jnp.exp(sink_col - m)
        ov = (_dot(p.astype(BF16), vc.astype(BF16)) + p_self * v_row) / den
        for c in range(B_HEADS // 2):
            kvh = (2 * c) // B_GROUP
            o_a = ov[2 * c:2 * c + 1, :]
            o_b = ov[2 * c + 1:2 * c + 2, :]
            if kvh == 0:
                o_b = pltpu.roll(o_b, B_DH, 1)
            else:
                o_a = pltpu.roll(o_a, B_DH, 1)
            ob_scr[pl.ds(b, 1), 128 * c:128 * (c + 1)] = jnp.where(lane1_lo, o_a, o_b)
        knew_ref[b, 0:WINDOW - 1, :] = ck_ref[b, 1:WINDOW, :]
        knew_ref[b, WINDOW - 1:WINDOW, :] = k_row
        vnew_ref[b, 0:WINDOW - 1, :] = cv_ref[b, 1:WINDOW, :]
        vnew_ref[b, WINDOW - 1:WINDOW, :] = v_row
        return carry

    for b in range(bt):
        body(b, 0)

    hg = hg_ref[...]
    for hd in range(A_HEADS):
        hs = slice(hd * A_DK, (hd + 1) * A_DK)
        gate = _silu(z_ref[:, OFF_G + hd * A_DV:OFF_G + (hd + 1) * A_DV])
        mix_ref[:, hs] = _rms(o_scr[:, hs], hg) * gate
    mix_ref[:, A_WIDTH:A_WIDTH + B_WIDTH] = ob_scr[...]

    gl = _gelu_tanh(z_ref[:, OFF_C:OFF_C + 2 * C_WIDTH])
    vn = _layer_norm(gl[:, C_WIDTH:], lng_ref[...], lnb_ref[...])
    vn_ref[...] = vn
    mix_ref[:, A_WIDTH + B_WIDTH:] = gl[:, 0:C_WIDTH] * (w0_ref[...] * vn + b0_ref[...])


def _mix_decode(z, lbl, hst, ck, cv, hg, qg2, kg2, sink_col, lng, lnb, w0, b0, layer):
    m = z.shape[0]
    bt = DEC_TILE
    depth = lbl.shape[0]
    full = lambda *shape: pl.BlockSpec(shape, lambda i: (0,) * len(shape))
    return pl.pallas_call(
        functools.partial(_mix_decode_kernel, layer=layer),
        grid=(m // bt,),
        in_specs=[
            pl.BlockSpec((bt, D_IN), lambda i: (i, 0)),
            full(depth, A_WIDTH),
            pl.BlockSpec((bt, A_HEADS, A_DK, A_DV), lambda i: (i, 0, 0, 0)),
            pl.BlockSpec((bt, WINDOW, KV_WIDTH), lambda i: (i, 0, 0)),
            pl.BlockSpec((bt, WINDOW, KV_WIDTH), lambda i: (i, 0, 0)),
            full(1, A_DV),
            full(1, 2 * B_DH),
            full(1, 2 * B_DH),
            full(B_HEADS, 1),
            full(1, C_WIDTH),
            full(1, C_WIDTH),
            full(1, C_WIDTH),
            full(1, C_WIDTH),
        ],
        out_specs=[
            pl.BlockSpec((bt, D_MODEL), lambda i: (i, 0)),
            pl.BlockSpec((bt, A_HEADS, A_DK, A_DV), lambda i: (i, 0, 0, 0)),
            pl.BlockSpec((bt, WINDOW, KV_WIDTH), lambda i: (i, 0, 0)),
            pl.BlockSpec((bt, WINDOW, KV_WIDTH), lambda i: (i, 0, 0)),
            pl.BlockSpec((bt, C_WIDTH), lambda i: (i, 0)),
        ],
        out_shape=[
            jax.ShapeDtypeStruct((m, D_MODEL), F32),
            jax.ShapeDtypeStruct((m, A_HEADS, A_DK, A_DV), F32),
            jax.ShapeDtypeStruct((m, WINDOW, KV_WIDTH), F32),
            jax.ShapeDtypeStruct((m, WINDOW, KV_WIDTH), F32),
            jax.ShapeDtypeStruct((m, C_WIDTH), F32),
        ],
        scratch_shapes=[
            pltpu.VMEM((bt, A_WIDTH), F32),
            pltpu.VMEM((bt, A_WIDTH), F32),
            pltpu.VMEM((bt, A_WIDTH), F32),
            pltpu.VMEM((bt, A_WIDTH), F32),
            pltpu.VMEM((bt, B_WIDTH), F32),
            pltpu.VMEM((bt, KV_WIDTH), F32),
            pltpu.VMEM((bt, B_WIDTH), F32),
        ],
        compiler_params=pltpu.CompilerParams(dimension_semantics=("parallel",), vmem_limit_bytes=VMEM_LIMIT),
        name="mix_decode",
    )(z, lbl, hst, ck, cv, hg, qg2, kg2, sink_col, lng, lnb, w0, b0)


def kernel(x_prompt, x_sample, state_hgrn, cache_swa_k, cache_swa_v, state_ffn_conv, p_prompt, p_sample, norm1_g, w_in, hgrn_lb_logits, hgrn_norm_g, q_norm_g, k_norm_g, swa_sinks, gmlp_ln_g, gmlp_ln_b, gmlp_ws, gmlp_bs, w_out, norm2_g, w_up, conv_w, conv_b, w_down, ple_norm_g, w_ple_gate, w_ple_proj):
    depth = w_in.shape[0]
    n_seq, seq, _ = x_prompt.shape
    n_dec = x_sample.shape[0]
    xp = x_prompt.reshape(n_seq * seq, D_MODEL)
    xs = x_sample.reshape(n_dec, D_MODEL)
    hp, hs, kp, vp, ks, vs, gs, cp, cs = [], [], [], [], [], [], [], [], []
    for l in range(depth):
        wi = w_in[l].astype(BF16)
        wo = w_out[l].astype(BF16)
        wu = w_up[l].astype(BF16)
        wd = w_down[l].astype(BF16)
        wg = w_ple_gate[l].astype(BF16)
        wp = w_ple_proj[l].astype(BF16)
        n1 = norm1_g[l].reshape(1, D_MODEL)
        n2 = norm2_g[l].reshape(1, D_MODEL)
        n3 = ple_norm_g[l].reshape(1, D_MODEL)
        hg = hgrn_norm_g[l].reshape(1, A_DV)
        qg2 = jnp.tile(q_norm_g[l], 2).reshape(1, 2 * B_DH)
        kg2 = jnp.tile(k_norm_g[l], 2).reshape(1, 2 * B_DH)
        lng = gmlp_ln_g[l].reshape(1, C_WIDTH)
        lnb = gmlp_ln_b[l].reshape(1, C_WIDTH)
        cb = conv_b[l].reshape(1, 2 * D_FF)

        z = _in_proj(xp, n1, wi, 256)
        mix, h_new, k_new, v_new = _mix_prompt(
            z, hgrn_lb_logits, hg, qg2, kg2, jnp.repeat(swa_sinks[l], BLK).reshape(1, B_HEADS * BLK), lng, lnb,
            gmlp_ws[l], gmlp_bs[l].T, n_seq, l)
        x1 = _out_proj(mix, xp, wo, 512)
        x2, cg, cv = _ffn_prompt(x1, n2, wu, conv_w[l], cb, wd, n_seq, 512)
        xp = _ple(x2, p_prompt[l].reshape(n_seq * seq, PLE_DIM), n3, wg, wp, 512)
        hp.append(h_new)
        kp.append(k_new.reshape(n_seq, WINDOW, B_KV_HEADS, B_DH))
        vp.append(v_new.reshape(n_seq, WINDOW, B_KV_HEADS, B_DH))
        last = lambda t: t.reshape(n_seq, -1, CONV_W - 1, D_FF)[:, -1]
        cp.append(jnp.concatenate([last(cg), last(cv)], axis=-1))

        zs = _in_proj(xs, n1, wi, n_dec)
        w0 = jnp.repeat(gmlp_ws[l][:, 0, 0], C_DG).reshape(1, C_WIDTH)
        b0 = jnp.repeat(gmlp_bs[l][:, 0], C_DG).reshape(1, C_WIDTH)
        mix_s, h_new, k_new, v_new, vn = _mix_decode(
            zs, hgrn_lb_logits, state_hgrn[l],
            cache_swa_k[l].reshape(n_dec, WINDOW, KV_WIDTH), cache_swa_v[l].reshape(n_dec, WINDOW, KV_WIDTH),
            hg, qg2, kg2, swa_sinks[l].reshape(B_HEADS, 1), lng, lnb, w0, b0, l)
        x1s = _out_proj(mix_s, xs, wo, n_dec)
        st = state_ffn_conv[l]
        x2s, upg, upv = _ffn_decode(x1s, n2, wu, conv_w[l], cb, st[:, 0, :], st[:, 1, :], wd)
        xs = _ple(x2s, p_sample[l].reshape(n_dec, PLE_DIM), n3, wg, wp, n_dec)
        hs.append(h_new)
        ks.append(k_new.reshape(n_dec, WINDOW, B_KV_HEADS, B_DH))
        vs.append(v_new.reshape(n_dec, WINDOW, B_KV_HEADS, B_DH))
        gs.append(vn.reshape(n_dec, 1, C_GROUPS, C_DG))
        cs.append(jnp.stack([st[:, 1, :], jnp.concatenate([upg, upv], axis=-1)], axis=1))

    return (xp.reshape(n_seq, seq, D_MODEL), xs.reshape(n_dec, 1, D_MODEL),
            jnp.stack(hp), jnp.stack(hs), jnp.stack(kp), jnp.stack(vp), jnp.stack(ks), jnp.stack(vs),
            jnp.stack(gs), jnp.stack(cp), jnp.stack(cs))
```

```python
import functools

import numpy as np
import jax
import jax.numpy as jnp
from jax import lax
from jax.experimental import pallas as pl
from jax.experimental.pallas import tpu as pltpu

D_MODEL = 2048
PLE_DIM = 256
A_DK = 128
A_DV = 128
A_HEADS = 8
A_WIDTH = A_HEADS * A_DV
B_DH = 64
B_HEADS = 8
B_KV_HEADS = 2
B_GROUP = B_HEADS // B_KV_HEADS
B_WIDTH = B_HEADS * B_DH
WINDOW = 128
C_DG = 128
C_GROUPS = 4
C_WIDTH = C_GROUPS * C_DG
C_CHUNK = 128
D_FF = 5632
CONV_W = 3
EPS = 1e-6
D_IN = 4 * A_WIDTH + B_WIDTH + 2 * B_KV_HEADS * B_DH + 2 * C_WIDTH
KV_WIDTH = B_KV_HEADS * B_DH
OFF_Q, OFF_F, OFF_I, OFF_G = 0, A_WIDTH, 2 * A_WIDTH, 3 * A_WIDTH
OFF_SQ = 4 * A_WIDTH
OFF_SK = OFF_SQ + B_WIDTH
OFF_SV = OFF_SK + KV_WIDTH
OFF_C = OFF_SV + KV_WIDTH

BLK = 128
N_LEVELS = 7
VMEM_LIMIT = 56 * 1024 * 1024

F32 = jnp.float32
BF16 = jnp.bfloat16

_NT = (((1,), (1,)), ((), ()))
_TN = (((0,), (0,)), ((), ()))


def _dot(a, b):
    return jnp.dot(a, b, preferred_element_type=F32)


def _dot_nt(a, b):
    return lax.dot_general(a, b, _NT, preferred_element_type=F32)


_TRI = np.tril(np.ones((BLK, BLK), np.float32))
LOG2E = np.float32(1.4426950408889634)


def _sigmoid(x):
    return jax.nn.sigmoid(x)


def _silu(x):
    return x * jax.nn.sigmoid(x)


def _logaddexp(a, b):
    return jnp.maximum(a, b) + jnp.log(1.0 + jnp.exp(-jnp.abs(a - b)))


def _gelu_tanh(x):
    c = np.float32(np.sqrt(2.0 / np.pi))
    return 0.5 * x * (1.0 + jnp.tanh(c * (x + np.float32(0.044715) * (x * x * x))))


def _rms(x, g):
    return x * lax.rsqrt(jnp.mean(x * x, axis=-1, keepdims=True) + EPS) * g


def _forget_gate(lbl_ref, layer, zf):
    lg = lbl_ref[...]
    e = jnp.exp(lg - jnp.max(lg, axis=0, keepdims=True))
    tot = jnp.sum(e, axis=0, keepdims=True)
    lb = jnp.zeros_like(tot)
    for j in range(1, layer + 1):
        lb = lb + e[j:j + 1, :] / tot
    t = jnp.exp(-jnp.abs(zf))
    u = 1.0 + t
    log_sig = jnp.minimum(zf, 0.0) - jnp.log(u)
    sig_neg = jnp.where(zf >= 0.0, t, 1.0) / u
    logf = _logaddexp(jnp.log(lb), jnp.log1p(-lb) + log_sig)
    return logf, (1.0 - lb) * sig_neg


def _segment_rms(x, g2, lane_lo):
    x2 = x * x
    s_lo = jnp.sum(jnp.where(lane_lo, x2, 0.0), axis=-1, keepdims=True)
    s_hi = jnp.sum(jnp.where(lane_lo, 0.0, x2), axis=-1, keepdims=True)
    inv = jnp.where(lane_lo, lax.rsqrt(s_lo / B_DH + EPS), lax.rsqrt(s_hi / B_DH + EPS))
    return x * inv * g2


def _layer_norm(x, g, b):
    xc = x - jnp.mean(x, axis=-1, keepdims=True)
    return xc * lax.rsqrt(jnp.mean(xc * xc, axis=-1, keepdims=True) + EPS) * g + b


def _in_proj_kernel(x_ref, g_ref, w_ref, o_ref):
    h = _rms(x_ref[...], g_ref[...]).astype(BF16)
    n = o_ref.shape[1]
    step = 512
    for c0 in range(0, n, step):
        c1 = min(c0 + step, n)
        o_ref[:, c0:c1] = _dot(h, w_ref[:, c0:c1])


def _in_proj(x, g, w, tm):
    m = x.shape[0]
    n = w.shape[1]
    return pl.pallas_call(
        _in_proj_kernel,
        grid=(m // tm,),
        in_specs=[
            pl.BlockSpec((tm, D_MODEL), lambda i: (i, 0)),
            pl.BlockSpec((1, D_MODEL), lambda i: (0, 0)),
            pl.BlockSpec((D_MODEL, n), lambda i: (0, 0), pipeline_mode=pl.Buffered(1)),
        ],
        out_specs=pl.BlockSpec((tm, n), lambda i: (i, 0)),
        out_shape=jax.ShapeDtypeStruct((m, n), F32),
        compiler_params=pltpu.CompilerParams(dimension_semantics=("parallel",), vmem_limit_bytes=VMEM_LIMIT),
        name="in_proj",
    )(x, g, w)


def _out_proj_kernel(mix_ref, x_ref, w_ref, o_ref):
    o_ref[...] = x_ref[...] + _dot(mix_ref[...].astype(BF16), w_ref[...])


def _out_proj(mix, x, w, tm):
    m = x.shape[0]
    return pl.pallas_call(
        _out_proj_kernel,
        grid=(m // tm,),
        in_specs=[
            pl.BlockSpec((tm, D_MODEL), lambda i: (i, 0)),
            pl.BlockSpec((tm, D_MODEL), lambda i: (i, 0)),
            pl.BlockSpec((D_MODEL, D_MODEL), lambda i: (0, 0), pipeline_mode=pl.Buffered(1)),
        ],
        out_specs=pl.BlockSpec((tm, D_MODEL), lambda i: (i, 0)),
        out_shape=jax.ShapeDtypeStruct((m, D_MODEL), F32),
        compiler_params=pltpu.CompilerParams(dimension_semantics=("parallel",), vmem_limit_bytes=VMEM_LIMIT),
        name="out_proj",
    )(mix, x, w)


def _ple_kernel(x_ref, p_ref, g_ref, wg_ref, wp_ref, o_ref):
    x = x_ref[...]
    h = _rms(x, g_ref[...]).astype(BF16)
    gate = _sigmoid(_dot(h, wg_ref[...]))
    o_ref[...] = x + gate * _dot(p_ref[...].astype(BF16), wp_ref[...])


def _ple(x, p, layer, g, wg, wp, tm):
    m = x.shape[0]
    return pl.pallas_call(
        _ple_kernel,
        grid=(m // tm,),
        in_specs=[
            pl.BlockSpec((tm, D_MODEL), lambda i: (i, 0)),
            pl.BlockSpec((None, tm, PLE_DIM), lambda i: (layer, i, 0)),
            pl.BlockSpec((1, D_MODEL), lambda i: (0, 0)),
            pl.BlockSpec((D_MODEL, D_MODEL), lambda i: (0, 0), pipeline_mode=pl.Buffered(1)),
            pl.BlockSpec((PLE_DIM, D_MODEL), lambda i: (0, 0), pipeline_mode=pl.Buffered(1)),
        ],
        out_specs=pl.BlockSpec((tm, D_MODEL), lambda i: (i, 0)),
        out_shape=jax.ShapeDtypeStruct((m, D_MODEL), F32),
        compiler_params=pltpu.CompilerParams(dimension_semantics=("parallel",), vmem_limit_bytes=VMEM_LIMIT),
        name="ple",
    )(x, p, g, wg, wp)


FFN_TN = 512
FFN_NJ = D_FF // FFN_TN
CARRY_ROWS = 8


def _ffn_prompt_kernel(x_ref, g_ref, wug_ref, wuv_ref, cwg_ref, cwv_ref, cbg_ref, cbv_ref, wd_ref,
                       o_ref, cg_ref, cv_ref, h_scr, acc_scr, carry_scr, bufg, bufv, act0, act1):
    i = pl.program_id(1)
    j = pl.program_id(2)
    tm = x_ref.shape[0]
    act = (act0, act1)

    jc = jnp.minimum(j, FFN_NJ - 1)
    first_tile = i == 0
    bufg[0:CARRY_ROWS, :] = jnp.where(first_tile, 0.0, carry_scr[0, jc])
    bufv[0:CARRY_ROWS, :] = jnp.where(first_tile, 0.0, carry_scr[1, jc])

    def up_conv(part):
        w_ref, cw_ref, cb_ref, buf, last_ref = ((wug_ref, cwg_ref, cbg_ref, bufg, cg_ref),
                                                (wuv_ref, cwv_ref, cbv_ref, bufv, cv_ref))[part]
        buf[CARRY_ROWS:, :] = _dot(h_scr[...], w_ref[...])
        last_ref[0] = buf[CARRY_ROWS + tm - (CONV_W - 1):, :]
        cw = cw_ref[...]
        out = cb_ref[...]
        for tap in range(CONV_W):
            lo = CARRY_ROWS - (CONV_W - 1) + tap
            out = out + cw[tap:tap + 1, :] * buf[lo:lo + tm, :]
        return out

    def gated_act(slot):
        gate = up_conv(0)
        act[slot][...] = (_silu(gate) * up_conv(1)).astype(BF16)

    def down_proj(slot):
        acc_scr[...] += _dot(act[slot][...], wd_ref[...])

    @pl.when(j == 0)
    def _():
        h_scr[...] = _rms(x_ref[...], g_ref[...]).astype(BF16)
        acc_scr[...] = jnp.zeros_like(acc_scr)
        gated_act(0)

    for parity in range(2):
        @pl.when((j > 0) & (j < FFN_NJ) & (j % 2 == parity))
        def _():
            gated_act(parity)
            down_proj(1 - parity)

    @pl.when(j == FFN_NJ)
    def _():
        down_proj((FFN_NJ - 1) % 2)
        o_ref[...] = x_ref[...] + acc_scr[...]

    @pl.when(j < FFN_NJ)
    def _():
        carry_scr[0, j] = bufg[tm:tm + CARRY_ROWS, :]
        carry_scr[1, j] = bufv[tm:tm + CARRY_ROWS, :]


def _ffn_prompt(x, g, w_up, conv_w, conv_b, w_down, n_seq, tm):
    m = x.shape[0]
    n_i = m // n_seq // tm
    tn = FFN_TN
    nj = FFN_NJ
    row = lambda b, i, j: (b * n_i + i, 0)
    up_j = lambda j: jnp.minimum(j, nj - 1)
    dn_j = lambda j: jnp.maximum(j - 1, 0)
    return pl.pallas_call(
        _ffn_prompt_kernel,
        grid=(n_seq, n_i, nj + 1),
        in_specs=[
            pl.BlockSpec((tm, D_MODEL), row),
            pl.BlockSpec((1, D_MODEL), lambda b, i, j: (0, 0)),
            pl.BlockSpec((D_MODEL, tn), lambda b, i, j: (0, up_j(j))),
            pl.BlockSpec((D_MODEL, tn), lambda b, i, j: (0, up_j(j) + nj)),
            pl.BlockSpec((CONV_W, tn), lambda b, i, j: (0, up_j(j))),
            pl.BlockSpec((CONV_W, tn), lambda b, i, j: (0, up_j(j) + nj)),
            pl.BlockSpec((1, tn), lambda b, i, j: (0, up_j(j))),
            pl.BlockSpec((1, tn), lambda b, i, j: (0, up_j(j) + nj)),
            pl.BlockSpec((tn, D_MODEL), lambda b, i, j: (dn_j(j), 0)),
        ],
        out_specs=[
            pl.BlockSpec((tm, D_MODEL), row),
            pl.BlockSpec((1, CONV_W - 1, tn), lambda b, i, j: (b * n_i + i, 0, up_j(j))),
            pl.BlockSpec((1, CONV_W - 1, tn), lambda b, i, j: (b * n_i + i, 0, up_j(j))),
        ],
        out_shape=[
            jax.ShapeDtypeStruct((m, D_MODEL), F32),
            jax.ShapeDtypeStruct((n_seq * n_i, CONV_W - 1, D_FF), F32),
            jax.ShapeDtypeStruct((n_seq * n_i, CONV_W - 1, D_FF), F32),
        ],
        scratch_shapes=[
            pltpu.VMEM((tm, D_MODEL), BF16),
            pltpu.VMEM((tm, D_MODEL), F32),
            pltpu.VMEM((2, nj, CARRY_ROWS, tn), F32),
            pltpu.VMEM((tm + CARRY_ROWS, tn), F32),
            pltpu.VMEM((tm + CARRY_ROWS, tn), F32),
            pltpu.VMEM((tm, tn), BF16),
            pltpu.VMEM((tm, tn), BF16),
        ],
        compiler_params=pltpu.CompilerParams(
            dimension_semantics=("arbitrary", "arbitrary", "arbitrary"), vmem_limit_bytes=VMEM_LIMIT),
        name="ffn_prompt",
    )(x, g, w_up, w_up, conv_w, conv_w, conv_b, conv_b, w_down)


def _ffn_decode_kernel(x_ref, g_ref, wug_ref, wuv_ref, cwg_ref, cwv_ref, cbg_ref, cbv_ref,
                       s0g_ref, s1g_ref, s0v_ref, s1v_ref, wd_ref,
                       o_ref, upg_ref, upv_ref, h_scr, acc_scr):
    j = pl.program_id(0)

    @pl.when(j == 0)
    def _():
        h_scr[...] = _rms(x_ref[...], g_ref[...]).astype(BF16)
        acc_scr[...] = jnp.zeros_like(acc_scr)

    h = h_scr[...]

    def conv(w_ref, cw_ref, cb_ref, s0_ref, s1_ref, up_ref):
        up = _dot(h, w_ref[...])
        up_ref[...] = up
        cw = cw_ref[...]
        out = cb_ref[...] + cw[0:1, :] * s0_ref[...]
        out = out + cw[1:2, :] * s1_ref[...]
        return out + cw[2:3, :] * up

    gate = conv(wug_ref, cwg_ref, cbg_ref, s0g_ref, s1g_ref, upg_ref)
    val = conv(wuv_ref, cwv_ref, cbv_ref, s0v_ref, s1v_ref, upv_ref)
    act = (_silu(gate) * val).astype(BF16)
    acc_scr[...] += _dot(act, wd_ref[...])

    @pl.when(j == FFN_NJ - 1)
    def _():
        o_ref[...] = x_ref[...] + acc_scr[...]


def _ffn_decode(x, g, w_up, conv_w, conv_b, s0, s1, w_down):
    m = x.shape[0]
    tn = FFN_TN
    nj = FFN_NJ
    lo = lambda j: (0, j)
    hi = lambda j: (0, j + nj)
    return pl.pallas_call(
        _ffn_decode_kernel,
        grid=(nj,),
        in_specs=[
            pl.BlockSpec((m, D_MODEL), lambda j: (0, 0)),
            pl.BlockSpec((1, D_MODEL), lambda j: (0, 0)),
            pl.BlockSpec((D_MODEL, tn), lo),
            pl.BlockSpec((D_MODEL, tn), hi),
            pl.BlockSpec((CONV_W, tn), lo),
            pl.BlockSpec((CONV_W, tn), hi),
            pl.BlockSpec((1, tn), lo),
            pl.BlockSpec((1, tn), hi),
            pl.BlockSpec((m, tn), lo),
            pl.BlockSpec((m, tn), lo),
            pl.BlockSpec((m, tn), hi),
            pl.BlockSpec((m, tn), hi),
            pl.BlockSpec((tn, D_MODEL), lambda j: (j, 0)),
        ],
        out_specs=[
            pl.BlockSpec((m, D_MODEL), lambda j: (0, 0)),
            pl.BlockSpec((m, tn), lo),
            pl.BlockSpec((m, tn), lo),
        ],
        out_shape=[
            jax.ShapeDtypeStruct((m, D_MODEL), F32),
            jax.ShapeDtypeStruct((m, D_FF), F32),
            jax.ShapeDtypeStruct((m, D_FF), F32),
        ],
        scratch_shapes=[pltpu.VMEM((m, D_MODEL), BF16), pltpu.VMEM((m, D_MODEL), F32)],
        compiler_params=pltpu.CompilerParams(dimension_semantics=("arbitrary",), vmem_limit_bytes=VMEM_LIMIT),
        name="ffn_decode",
    )(x, g, w_up, w_up, conv_w, conv_w, conv_b, conv_b, s0, s1, s0, s1, w_down)


def _mix_prompt_kernel(z_ref, lbl_ref, tri_ref, hg_ref, qg_ref, kg_ref, sink_ref, lng_ref, lnb_ref, ws_ref, bst_ref,
                       mix_ref, hst_ref, klast_ref, vlast_ref,
                       st_scr, kprev, vprev, q_scr, k_scr, b_scr, *, layer, n_blocks):
    i = pl.program_id(1)

    @pl.when(i == 0)
    def _():
        st_scr[...] = jnp.zeros_like(st_scr)
        kprev[...] = jnp.zeros_like(kprev)
        vprev[...] = jnp.zeros_like(vprev)

    row = lax.broadcasted_iota(jnp.int32, (BLK, BLK), 0)
    col = lax.broadcasted_iota(jnp.int32, (BLK, BLK), 1)

    logf, kk = _forget_gate(lbl_ref, layer, z_ref[:, OFF_F:OFF_F + A_WIDTH])
    lf2 = logf * LOG2E
    l_hi = lf2.astype(BF16)
    l_lo = (lf2 - l_hi.astype(F32)).astype(BF16)
    tri = tri_ref[...]
    for c0 in range(0, A_WIDTH, 256):
        b_scr[:, c0:c0 + 256] = _dot(tri, l_hi[:, c0:c0 + 256]) + _dot(tri, l_lo[:, c0:c0 + 256])
    k_scr[...] = kk
    q_scr[...] = _silu(z_ref[:, OFF_Q:OFF_Q + A_WIDTH])

    split = jnp.where(row > col, row ^ col, 0)
    sub = lax.broadcasted_iota(jnp.int32, (BLK // 8, 8, A_DK), 1)
    hg = hg_ref[...]

    def level_factor(b, k):
        n = 1 << k
        if 2 * n >= 8:
            bk = b.reshape(BLK // (2 * n), 2 * n, A_DK)
            return jnp.exp2(-jnp.abs(bk - bk[:, n - 1:n, :])).reshape(BLK, A_DK)
        b3 = b.reshape(BLK // 8, 8, A_DK)
        if k == 0:
            r = jnp.where((sub & 1) == 1, pltpu.roll(b3, 1, 1), b3)
        else:
            m4 = sub & 3
            r = jnp.where(m4 == 0, pltpu.roll(b3, 7, 1),
                          jnp.where(m4 == 1, b3, jnp.where(m4 == 2, pltpu.roll(b3, 1, 1), pltpu.roll(b3, 2, 1))))
        return jnp.exp2(-jnp.abs(b3 - r)).reshape(BLK, A_DK)

    def head(hd):
        off = pl.multiple_of(hd * A_DK, A_DK)
        hs = pl.ds(off, A_DK)
        b = b_scr[:, hs]
        qh = q_scr[:, hs]
        kh = k_scr[:, hs]
        vb = z_ref[:, pl.ds(OFF_I + off, A_DV)].astype(BF16)
        sc = jnp.where(row == col, _dot_nt(qh.astype(BF16), kh.astype(BF16)), 0.0)
        for k in range(N_LEVELS):
            e_l = level_factor(b, k)
            sc = jnp.where((split >> k) == 1, _dot_nt((qh * e_l).astype(BF16), (kh * e_l).astype(BF16)), sc)
        st = st_scr[hd]
        b_last = b[BLK - 1:BLK, :]
        o = _dot(sc.astype(BF16), vb) + _dot_nt((qh * jnp.exp2(b)).astype(BF16), st.astype(BF16))
        upd = lax.dot_general(vb, (kh * jnp.exp2(b_last - b)).astype(BF16), _TN, preferred_element_type=F32)
        st_scr[hd] = st * jnp.exp2(b_last) + upd
        gate = _silu(z_ref[:, pl.ds(OFF_G + off, A_DV)])
        mix_ref[:, hs] = (_rms(o, hg) * gate).astype(mix_ref.dtype)

    def two_heads(it, carry):
        head(2 * it)
        head(2 * it + 1)
        return carry

    lax.fori_loop(0, A_HEADS // 2, two_heads, 0)

    lane_lo = lax.broadcasted_iota(jnp.int32, (BLK, 128), 1) < B_DH
    kn = _segment_rms(z_ref[:, OFF_SK:OFF_SK + KV_WIDTH], kg_ref[...], lane_lo)
    vv = z_ref[:, OFF_SV:OFF_SV + KV_WIDTH]
    kcat = jnp.concatenate([kprev[...], kn], axis=0).astype(BF16)
    vcat_t = jnp.concatenate([vprev[...].T, vv.T], axis=1).astype(BF16)
    kprev[...] = kn
    vprev[...] = vv
    klast_ref[0] = kn
    vlast_ref[0] = vv

    kj = lax.broadcasted_iota(jnp.int32, (2 * BLK, B_GROUP * BLK), 0)
    qi = lax.broadcasted_iota(jnp.int32, (2 * BLK, B_GROUP * BLK), 1) & (BLK - 1)
    first_key = jnp.where(i > 0, 0, BLK)
    allowed = (kj > qi) & (kj <= qi + WINDOW) & (kj >= first_key)
    scale = np.float32(B_DH ** -0.5)
    for g in range(B_KV_HEADS):
        kv_lanes = lane_lo if g == 0 else jnp.logical_not(lane_lo)
        qs = []
        for c in range(2 * g, 2 * g + 2):
            qn = _segment_rms(z_ref[:, OFF_SQ + 128 * c:OFF_SQ + 128 * (c + 1)], qg_ref[...], lane_lo) * scale
            qn_r = pltpu.roll(qn, B_DH, 1)
            for half in range(2):
                qs.append(jnp.where(kv_lanes, qn if half == g else qn_r, 0.0).astype(BF16))
        q4 = jnp.concatenate(qs, axis=0)
        s = jnp.where(allowed, _dot_nt(kcat, q4), -jnp.inf)
        sk = sink_ref[:, B_GROUP * BLK * g:B_GROUP * BLK * (g + 1)]
        m = jnp.maximum(jnp.max(s, axis=0, keepdims=True), sk)
        p = jnp.exp(s - m)
        inv = 1.0 / (jnp.sum(p, axis=0, keepdims=True) + jnp.exp(sk - m))
        ot = _dot(vcat_t[B_DH * g:B_DH * (g + 1), :], p.astype(BF16)) * inv
        for cc in range(2):
            pair = jnp.concatenate([ot[:, 2 * cc * BLK:(2 * cc + 1) * BLK], ot[:, (2 * cc + 1) * BLK:(2 * cc + 2) * BLK]], axis=0)
            c = 2 * g + cc
            mix_ref[:, A_WIDTH + 128 * c:A_WIDTH + 128 * (c + 1)] = pair.T.astype(mix_ref.dtype)

    gl = _gelu_tanh(z_ref[:, OFF_C:OFF_C + 2 * C_WIDTH])
    u = gl[:, 0:C_WIDTH]
    vn = _layer_norm(gl[:, C_WIDTH:], lng_ref[...], lnb_ref[...])
    bst = bst_ref[...]
    for gi in range(C_GROUPS):
        gs = slice(gi * C_DG, (gi + 1) * C_DG)
        w = jnp.where(col <= row, ws_ref[gi], 0.0).astype(BF16)
        mixed = _dot(w, vn[:, gs].astype(BF16)) + bst[:, gi:gi + 1]
        mix_ref[:, A_WIDTH + B_WIDTH + gi * C_DG:A_WIDTH + B_WIDTH + (gi + 1) * C_DG] = (u[:, gs] * mixed).astype(mix_ref.dtype)

    @pl.when(i == n_blocks - 1)
    def _():
        for hd in range(A_HEADS):
            hst_ref[0, hd] = st_scr[hd].T


def _mix_prompt(z, lbl, hg, qg2, kg2, sinks, lng, lnb, ws, bst, n_seq, layer):
    m = z.shape[0]
    n_blocks = m // n_seq // BLK
    depth = lbl.shape[0]
    full = lambda *shape: pl.BlockSpec(shape, lambda b, i: (0,) * len(shape))
    return pl.pallas_call(
        functools.partial(_mix_prompt_kernel, layer=layer, n_blocks=n_blocks),
        grid=(n_seq, n_blocks),
        in_specs=[
            pl.BlockSpec((BLK, D_IN), lambda b, i: (b * n_blocks + i, 0)),
            full(depth, A_WIDTH),
            full(BLK, BLK),
            full(1, A_DV),
            full(1, 2 * B_DH),
            full(1, 2 * B_DH),
            full(1, B_HEADS * BLK),
            full(1, C_WIDTH),
            full(1, C_WIDTH),
            full(C_GROUPS, C_CHUNK, C_CHUNK),
            full(C_CHUNK, C_GROUPS),
        ],
        out_specs=[
            pl.BlockSpec((BLK, D_MODEL), lambda b, i: (b * n_blocks + i, 0)),
            pl.BlockSpec((1, A_HEADS, A_DK, A_DV), lambda b, i: (b, 0, 0, 0)),
            pl.BlockSpec((1, WINDOW, KV_WIDTH), lambda b, i: (b, 0, 0)),
            pl.BlockSpec((1, WINDOW, KV_WIDTH), lambda b, i: (b, 0, 0)),
        ],
        out_shape=[
            jax.ShapeDtypeStruct((m, D_MODEL), BF16),
            jax.ShapeDtypeStruct((n_seq, A_HEADS, A_DK, A_DV), F32),
            jax.ShapeDtypeStruct((n_seq, WINDOW, KV_WIDTH), F32),
            jax.ShapeDtypeStruct((n_seq, WINDOW, KV_WIDTH), F32),
        ],
        scratch_shapes=[
            pltpu.VMEM((A_HEADS, A_DV, A_DK), F32),
            pltpu.VMEM((BLK, KV_WIDTH), F32),
            pltpu.VMEM((BLK, KV_WIDTH), F32),
            pltpu.VMEM((BLK, A_WIDTH), F32),
            pltpu.VMEM((BLK, A_WIDTH), F32),
            pltpu.VMEM((BLK, A_WIDTH), F32),
        ],
        compiler_params=pltpu.CompilerParams(
            dimension_semantics=("arbitrary", "arbitrary"), vmem_limit_bytes=VMEM_LIMIT),
        name="mix_prompt",
    )(z, lbl, jnp.asarray(_TRI, BF16), hg, qg2, kg2, sinks, lng, lnb, ws, bst)


DEC_TILE = 8


def _mix_decode_kernel(z_ref, lbl_ref, hst_ref, ck_ref, cv_ref, hg_ref, qg_ref, kg_ref, sink_ref, lng_ref, lnb_ref,
                       w0_ref, b0_ref, *rest, layer):
    (mix_ref, hnew_ref, knew_ref, vnew_ref, vn_ref,
     f_scr, k_scr, q_scr, o_scr, qn_scr, kn_scr, ob_scr) = rest[-12:]
    bt = z_ref.shape[0]
    eye = lax.broadcasted_iota(jnp.int32, (A_DK, A_DK), 0) == lax.broadcasted_iota(jnp.int32, (A_DK, A_DK), 1)
    lane8 = lax.broadcasted_iota(jnp.int32, (B_HEADS, 128), 1)
    row8 = lax.broadcasted_iota(jnp.int32, (B_HEADS, 128), 0)
    lane1_lo = lax.broadcasted_iota(jnp.int32, (1, 128), 1) < B_DH
    lane_lo = lax.broadcasted_iota(jnp.int32, (bt, 128), 1) < B_DH

    logf, kk = _forget_gate(lbl_ref, layer, z_ref[:, OFF_F:OFF_F + A_WIDTH])
    f_scr[...] = jnp.exp(logf)
    k_scr[...] = kk
    q_scr[...] = _silu(z_ref[:, OFF_Q:OFF_Q + A_WIDTH])

    scale = np.float32(B_DH ** -0.5)
    for c in range(B_HEADS // 2):
        qn_scr[:, 128 * c:128 * (c + 1)] = _segment_rms(
            z_ref[:, OFF_SQ + 128 * c:OFF_SQ + 128 * (c + 1)], qg_ref[...], lane_lo) * scale
    kn_scr[...] = _segment_rms(z_ref[:, OFF_SK:OFF_SK + KV_WIDTH], kg_ref[...], lane_lo)
    sink_col = sink_ref[...]

    def column(rowvec):
        return jnp.sum(jnp.where(eye, rowvec, 0.0), axis=1, keepdims=True)

    def body(b, carry):
        for hd in range(A_HEADS):
            hs = slice(hd * A_DK, (hd + 1) * A_DK)
            f_col = column(f_scr[pl.ds(b, 1), hs])
            k_col = column(k_scr[pl.ds(b, 1), hs])
            q_col = column(q_scr[pl.ds(b, 1), hs])
            v_row = z_ref[pl.ds(b, 1), OFF_I + hd * A_DV:OFF_I + (hd + 1) * A_DV]
            s_new = hst_ref[b, hd] * f_col + k_col * v_row
            hnew_ref[b, hd] = s_new
            o_scr[pl.ds(b, 1), hs] = jnp.sum(s_new * q_col, axis=0, keepdims=True)

        kc = ck_ref[b]
        vc = cv_ref[b]
        k_row = kn_scr[pl.ds(b, 1), :]
        v_row = z_ref[pl.ds(b, 1), OFF_SV:OFF_SV + KV_WIDTH]
        qp = jnp.zeros((B_HEADS, 128), F32)
        for c in range(B_HEADS // 2):
            kvh = (2 * c) // B_GROUP
            q_row = qn_scr[pl.ds(b, 1), 128 * c:128 * (c + 1)]
            q_rolled = pltpu.roll(q_row, B_DH, 1)
            kv_lanes = lane1_lo if kvh == 0 else jnp.logical_not(lane1_lo)
            for half in range(2):
                src = jnp.where(kv_lanes, q_row if half == kvh else q_rolled, 0.0)
                qp = jnp.where(row8 == 2 * c + half, src, qp)
        s = jnp.where(lane8 >= 1, _dot_nt(qp.astype(BF16), kc.astype(BF16)), -jnp.inf)
        s_self = jnp.sum(qp * k_row, axis=1, keepdims=True)
        m = jnp.maximum(jnp.maximum(jnp.max(s, axis=1, keepdims=True), s_self), sink_col)
        p = jnp.exp(s - m)
        p_self = jnp.exp(s_self - m)
        den = jnp.sum(p, axis=1, keepdims=True) + p_self + jnp.exp(sink_col - m)
        ov = (_dot(p.astype(BF16), vc.astype(BF16)) + p_self * v_row) / den
        for c in range(B_HEADS // 2):
            kvh = (2 * c) // B_GROUP
            o_a = ov[2 * c:2 * c + 1, :]
            o_b = ov[2 * c + 1:2 * c + 2, :]
            if kvh == 0:
                o_b = pltpu.roll(o_b, B_DH, 1)
            else:
                o_a = pltpu.roll(o_a, B_DH, 1)
            ob_scr[pl.ds(b, 1), 128 * c:128 * (c + 1)] = jnp.where(lane1_lo, o_a, o_b)
        knew_ref[b, 0:WINDOW - 1, :] = ck_ref[b, 1:WINDOW, :]
        knew_ref[b, WINDOW - 1:WINDOW, :] = k_row
        vnew_ref[b, 0:WINDOW - 1, :] = cv_ref[b, 1:WINDOW, :]
        vnew_ref[b, WINDOW - 1:WINDOW, :] = v_row
        return carry

    for b in range(bt):
        body(b, 0)

    hg = hg_ref[...]
    for hd in range(A_HEADS):
        hs = slice(hd * A_DK, (hd + 1) * A_DK)
        gate = _silu(z_ref[:, OFF_G + hd * A_DV:OFF_G + (hd + 1) * A_DV])
        mix_ref[:, hs] = _rms(o_scr[:, hs], hg) * gate
    mix_ref[:, A_WIDTH:A_WIDTH + B_WIDTH] = ob_scr[...]

    gl = _gelu_tanh(z_ref[:, OFF_C:OFF_C + 2 * C_WIDTH])
    vn = _layer_norm(gl[:, C_WIDTH:], lng_ref[...], lnb_ref[...])
    vn_ref[...] = vn
    mix_ref[:, A_WIDTH + B_WIDTH:] = gl[:, 0:C_WIDTH] * (w0_ref[...] * vn + b0_ref[...])


def _mix_decode(z, lbl, hst, ck, cv, hg, qg2, kg2, sink_col, lng, lnb, w0, b0, layer, stacked):
    m = z.shape[0]
    bt = DEC_TILE
    depth = lbl.shape[0]
    full = lambda *shape: pl.BlockSpec(shape, lambda i: (0,) * len(shape))
    state_spec = pl.BlockSpec((None, bt, A_HEADS, A_DK, A_DV), lambda i: (layer, i, 0, 0, 0))
    cache_spec = pl.BlockSpec((None, bt, WINDOW, KV_WIDTH), lambda i: (layer, i, 0, 0))
    n_in = 13
    carried = () if stacked is None else tuple(stacked)
    return pl.pallas_call(
        functools.partial(_mix_decode_kernel, layer=layer),
        grid=(m // bt,),
        in_specs=[
            pl.BlockSpec((bt, D_IN), lambda i: (i, 0)),
            full(depth, A_WIDTH),
            state_spec,
            cache_spec,
            cache_spec,
            full(1, A_DV),
            full(1, 2 * B_DH),
            full(1, 2 * B_DH),
            full(B_HEADS, 1),
            full(1, C_WIDTH),
            full(1, C_WIDTH),
            full(1, C_WIDTH),
            full(1, C_WIDTH),
        ] + [pl.BlockSpec(memory_space=pl.ANY)] * len(carried),
        out_specs=[
            pl.BlockSpec((bt, D_MODEL), lambda i: (i, 0)),
            state_spec,
            cache_spec,
            cache_spec,
            pl.BlockSpec((bt, C_WIDTH), lambda i: (i, 0)),
        ],
        out_shape=[
            jax.ShapeDtypeStruct((m, D_MODEL), F32),
            jax.ShapeDtypeStruct((depth, m, A_HEADS, A_DK, A_DV), F32),
            jax.ShapeDtypeStruct((depth, m, WINDOW, KV_WIDTH), F32),
            jax.ShapeDtypeStruct((depth, m, WINDOW, KV_WIDTH), F32),
            jax.ShapeDtypeStruct((m, C_WIDTH), F32),
        ],
        input_output_aliases={n_in + k: 1 + k for k in range(len(carried))},
        scratch_shapes=[
            pltpu.VMEM((bt, A_WIDTH), F32),
            pltpu.VMEM((bt, A_WIDTH), F32),
            pltpu.VMEM((bt, A_WIDTH), F32),
            pltpu.VMEM((bt, A_WIDTH), F32),
            pltpu.VMEM((bt, B_WIDTH), F32),
            pltpu.VMEM((bt, KV_WIDTH), F32),
            pltpu.VMEM((bt, B_WIDTH), F32),
        ],
        compiler_params=pltpu.CompilerParams(dimension_semantics=("parallel",), vmem_limit_bytes=VMEM_LIMIT),
        name="mix_decode",
    )(z, lbl, hst, ck, cv, hg, qg2, kg2, sink_col, lng, lnb, w0, b0, *carried)


def kernel(x_prompt, x_sample, state_hgrn, cache_swa_k, cache_swa_v, state_ffn_conv, p_prompt, p_sample, norm1_g, w_in, hgrn_lb_logits, hgrn_norm_g, q_norm_g, k_norm_g, swa_sinks, gmlp_ln_g, gmlp_ln_b, gmlp_ws, gmlp_bs, w_out, norm2_g, w_up, conv_w, conv_b, w_down, ple_norm_g, w_ple_gate, w_ple_proj):
    depth = w_in.shape[0]
    n_seq, seq, _ = x_prompt.shape
    n_dec = x_sample.shape[0]
    xp = x_prompt.reshape(n_seq * seq, D_MODEL)
    xs = x_sample.reshape(n_dec, D_MODEL)
    pp = p_prompt.reshape(depth, n_seq * seq, PLE_DIM)
    ps = p_sample.reshape(depth, n_dec, PLE_DIM)
    ck_all = cache_swa_k.reshape(depth, n_dec, WINDOW, KV_WIDTH)
    cv_all = cache_swa_v.reshape(depth, n_dec, WINDOW, KV_WIDTH)
    hp, kp, vp, gs, cp, cs = [], [], [], [], [], []
    dec_state = None
    for l in range(depth):
        wi = w_in[l].astype(BF16)
        wo = w_out[l].astype(BF16)
        wu = w_up[l].astype(BF16)
        wd = w_down[l].astype(BF16)
        wg = w_ple_gate[l].astype(BF16)
        wp = w_ple_proj[l].astype(BF16)
        n1 = norm1_g[l].reshape(1, D_MODEL)
        n2 = norm2_g[l].reshape(1, D_MODEL)
        n3 = ple_norm_g[l].reshape(1, D_MODEL)
        hg = hgrn_norm_g[l].reshape(1, A_DV)
        qg2 = jnp.tile(q_norm_g[l], 2).reshape(1, 2 * B_DH)
        kg2 = jnp.tile(k_norm_g[l], 2).reshape(1, 2 * B_DH)
        lng = gmlp_ln_g[l].reshape(1, C_WIDTH)
        lnb = gmlp_ln_b[l].reshape(1, C_WIDTH)
        cb = conv_b[l].reshape(1, 2 * D_FF)

        z = _in_proj(xp, n1, wi, 256)
        mix, h_new, k_new, v_new = _mix_prompt(
            z, hgrn_lb_logits, hg, qg2, kg2, jnp.repeat(swa_sinks[l], BLK).reshape(1, B_HEADS * BLK), lng, lnb,
            gmlp_ws[l], gmlp_bs[l].T, n_seq, l)
        x1 = _out_proj(mix, xp, wo, 512)
        x2, cg, cv = _ffn_prompt(x1, n2, wu, conv_w[l], cb, wd, n_seq, 512)
        xp = _ple(x2, pp, l, n3, wg, wp, 512)
        hp.append(h_new)
        kp.append(k_new.reshape(n_seq, WINDOW, B_KV_HEADS, B_DH))
        vp.append(v_new.reshape(n_seq, WINDOW, B_KV_HEADS, B_DH))
        last = lambda t: t.reshape(n_seq, -1, CONV_W - 1, D_FF)[:, -1]
        cp.append(jnp.concatenate([last(cg), last(cv)], axis=-1))

        zs = _in_proj(xs, n1, wi, n_dec)
        w0 = jnp.repeat(gmlp_ws[l][:, 0, 0], C_DG).reshape(1, C_WIDTH)
        b0 = jnp.repeat(gmlp_bs[l][:, 0], C_DG).reshape(1, C_WIDTH)
        mix_s, *dec_state, vn = _mix_decode(
            zs, hgrn_lb_logits, state_hgrn, ck_all, cv_all,
            hg, qg2, kg2, swa_sinks[l].reshape(B_HEADS, 1), lng, lnb, w0, b0, l, dec_state)
        x1s = _out_proj(mix_s, xs, wo, n_dec)
        st = state_ffn_conv[l]
        x2s, upg, upv = _ffn_decode(x1s, n2, wu, conv_w[l], cb, st[:, 0, :], st[:, 1, :], wd)
        xs = _ple(x2s, ps, l, n3, wg, wp, n_dec)
        gs.append(vn.reshape(n_dec, 1, C_GROUPS, C_DG))
        cs.append(jnp.stack([st[:, 1, :], jnp.concatenate([upg, upv], axis=-1)], axis=1))

    hs, ks, vs = dec_state
    cache_shape = (depth, n_dec, WINDOW, B_KV_HEADS, B_DH)
    return (xp.reshape(n_seq, seq, D_MODEL), xs.reshape(n_dec, 1, D_MODEL),
            jnp.stack(hp), hs, jnp.stack(kp), jnp.stack(vp), ks.reshape(cache_shape), vs.reshape(cache_shape),
            jnp.stack(gs), jnp.stack(cp), jnp.stack(cs))
```
